```python
import math
import jax, jax.numpy as jnp
from jax import lax
import numpy as np

D_MODEL = 1024
BATCH = 8
SEQ = 8192
DEPTH = 1

PLE_DIM = 256
D_FF = 2816
EPS = 1e-6
NSA_HEADS = 4
NSA_HEAD_DIM = 128
NSA_KV_DIM = 128
CMP_BLOCK = 32
CMP_STRIDE = 16
SLC_BLOCK = 64
SLC_TOPK = 16
WINDOW = 512
NSA_QBLOCK = 64
FORCE_BONUS = 1e4
NEG = -1e30
GLA_HEADS = 4
GLA_KEY_DIM = 64
GLA_VAL_DIM = 128
GLA_GATE_RANK = 16
GLA_GATE_NORM = 16.0
GLA_CHUNK = 64

NSA_WIDTH = NSA_HEADS * NSA_HEAD_DIM
GLA_WIDTH = GLA_HEADS * GLA_VAL_DIM
MIX_WIDTH = NSA_WIDTH + GLA_WIDTH
IN_SPLITS = (NSA_WIDTH,
             NSA_KV_DIM, NSA_KV_DIM,
             NSA_KV_DIM, NSA_KV_DIM,
             NSA_KV_DIM, NSA_KV_DIM,
             3 * NSA_HEADS,
             GLA_HEADS * GLA_KEY_DIM,
             GLA_HEADS * GLA_KEY_DIM,
             GLA_WIDTH,
             GLA_GATE_RANK,
             GLA_WIDTH)
IN_WIDTH = sum(IN_SPLITS)

kernel_name = "hymba_nsa_gla_macaron_block"


def _rmsnorm(x, g):
    x32 = x.astype(jnp.float32)
    y = x32 * lax.rsqrt(jnp.mean(x32 * x32, axis=-1, keepdims=True) + EPS)
    return (y * g.astype(jnp.float32)).astype(x.dtype)


def _swiglu(u, wg, wu, wd):
    return (jax.nn.silu(u @ wg) * (u @ wu)) @ wd


def _nsa(q, k_cmp, v_cmp, k_slc, v_slc, k_win, v_win, gate_logits,
         cpe_k, cw1_k, cw2_k, cpe_v, cw1_v, cw2_v):
    B, S, H, Dh = q.shape
    Dk = k_cmp.shape[-1]
    dt = q.dtype
    n_cmp = (S - CMP_BLOCK) // CMP_STRIDE + 1
    n_slc = S // SLC_BLOCK
    topk = min(SLC_TOPK, n_slc)

    blk_idx = jnp.arange(n_cmp)[:, None] * CMP_STRIDE + jnp.arange(CMP_BLOCK)[None, :]

    def compress(t, pe, w1, w2):
        blocks = t[:, blk_idx] + pe
        hid = jax.nn.silu(jnp.einsum('bnld,lde->bne', blocks, w1))
        return hid @ w2

    kc = compress(k_cmp, cpe_k, cw1_k, cw2_k)
    vc = compress(v_cmp, cpe_v, cw1_v, cw2_v)
    cmp_start = jnp.arange(n_cmp) * CMP_STRIDE
    cmp_end = cmp_start + CMP_BLOCK - 1
    slc_start = jnp.arange(n_slc) * SLC_BLOCK
    overlap = ((cmp_start[:, None] < slc_start[None, :] + SLC_BLOCK)
               & (cmp_start[:, None] + CMP_BLOCK > slc_start[None, :])).astype(jnp.float32)

    kb = k_slc.reshape(B, n_slc, SLC_BLOCK, Dk)
    vb = v_slc.reshape(B, n_slc, SLC_BLOCK, Dk)
    pad = jnp.zeros((B, WINDOW, Dk), k_win.dtype)
    kw_pad = jnp.concatenate([pad, k_win], axis=1)
    vw_pad = jnp.concatenate([pad, v_win], axis=1)
    qs = q * (NSA_HEAD_DIM ** -0.5)
    blk_ids = jnp.arange(n_slc)

    def block(qb_idx):
        q0 = qb_idx * NSA_QBLOCK
        qb = lax.dynamic_slice_in_dim(qs, q0, NSA_QBLOCK, axis=1)
        gb = lax.dynamic_slice_in_dim(gate_logits, q0, NSA_QBLOCK, axis=1)
        t = q0 + jnp.arange(NSA_QBLOCK)

        s_c = jnp.einsum('bqhd,bnd->bhqn', qb, kc).astype(jnp.float32)
        m_c = cmp_end[None, :] <= t[:, None]
        p_c = jax.nn.softmax(jnp.where(m_c, s_c, NEG), axis=-1) * m_c
        o_c = jnp.einsum('bhqn,bnd->bqhd', p_c.astype(vc.dtype), vc)

        imp = jnp.einsum('bhqn,ns->bqs', p_c, overlap)
        cur = t // SLC_BLOCK
        valid = blk_ids[None, :] <= cur[:, None]
        forced = ((blk_ids[None, :] == 0) | (blk_ids[None, :] == cur[:, None])
                  | (blk_ids[None, :] == cur[:, None] - 1))
        imp = jnp.where(valid, imp + jnp.where(forced, FORCE_BONUS, 0.0), -1.0)
        _, sel = lax.top_k(imp, topk)
        kg = jax.vmap(lambda kb_b, i_b: kb_b[i_b])(kb, sel)
        vg = jax.vmap(lambda vb_b, i_b: vb_b[i_b])(vb, sel)
        pos = sel[..., None] * SLC_BLOCK + jnp.arange(SLC_BLOCK)
        m_s = pos <= t[None, :, None, None]
        s_s = jnp.einsum('bqhd,bqnld->bhqnl', qb, kg).astype(jnp.float32)
        s_s = jnp.where(m_s[:, None], s_s, NEG)
        p_s = jax.nn.softmax(s_s.reshape(B, H, NSA_QBLOCK, -1), axis=-1).reshape(s_s.shape)
        o_s = jnp.einsum('bhqnl,bqnld->bqhd', p_s.astype(vg.dtype), vg)

        kw = lax.dynamic_slice_in_dim(kw_pad, q0, WINDOW + NSA_QBLOCK, axis=1)
        vw = lax.dynamic_slice_in_dim(vw_pad, q0, WINDOW + NSA_QBLOCK, axis=1)
        s_pos = q0 - WINDOW + jnp.arange(WINDOW + NSA_QBLOCK)
        diff = t[:, None] - s_pos[None, :]
        m_w = (diff >= 0) & (diff < WINDOW) & (s_pos[None, :] >= 0)
        s_w = jnp.einsum('bqhd,bkd->bhqk', qb, kw).astype(jnp.float32)
        p_w = jax.nn.softmax(jnp.where(m_w, s_w, NEG), axis=-1)
        o_w = jnp.einsum('bhqk,bkd->bqhd', p_w.astype(vw.dtype), vw)

        g = jax.nn.sigmoid(gb.astype(jnp.float32))
        out = (g[..., 0:1] * o_c.astype(jnp.float32) + g[..., 1:2] * o_s.astype(jnp.float32)
               + g[..., 2:3] * o_w.astype(jnp.float32))
        return out.astype(dt)

    outs = lax.map(block, jnp.arange(S // NSA_QBLOCK))
    return outs.transpose(1, 0, 2, 3, 4).reshape(B, S, H, Dh)


def _gla(q, k, v, g_log):
    B, S, H, dk = q.shape
    dv = v.shape[-1]
    dt = v.dtype
    C = GLA_CHUNK
    N = S // C

    def chunks(a):
        return a.astype(jnp.float32).reshape(B, N, C, H, -1).transpose(1, 0, 3, 2, 4)

    qc, kc, vc, gc = chunks(q), chunks(k), chunks(v), chunks(g_log)
    b = jnp.cumsum(gc, axis=3)
    b_last = b[:, :, :, -1:, :]
    q_dec = qc * jnp.exp(b) * (GLA_KEY_DIM ** -0.5)
    k_dec = kc * jnp.exp(-b)
    k_to_end = kc * jnp.exp(b_last - b)
    causal = jnp.tril(jnp.ones((C, C), jnp.float32))
    A = jnp.einsum('nbhid,nbhjd->nbhij', q_dec, k_dec) * causal
    o_intra = jnp.einsum('nbhij,nbhjv->nbhiv', A, vc)

    def step(state, inp):
        q_d, k_e, v_c, decay = inp
        o_inter = jnp.einsum('bhid,bhdv->bhiv', q_d, state)
        state = state * decay[:, :, 0, :, None] + jnp.einsum('bhjd,bhjv->bhdv', k_e, v_c)
        return state, o_inter

    state0 = jnp.zeros((B, H, dk, dv), jnp.float32)
    _, o_inter = lax.scan(step, state0, (q_dec, k_to_end, vc, jnp.exp(b_last)))
    o = o_intra + o_inter
    return o.transpose(1, 0, 3, 2, 4).reshape(B, S, H, dv).astype(dt)


def setup_inputs(seed: int = 0) -> dict:
    key = jax.random.key(seed)
    ks = iter(jax.random.split(key, 40))
    L, D, F = DEPTH, D_MODEL, D_FF

    def nrm(shape, scale):
        return jax.random.normal(next(ks), shape, jnp.float32) * scale

    def gain(shape):
        return 1.0 + 0.05 * jax.random.normal(next(ks), shape, jnp.float32)

    d = {}
    d["x"] = nrm((BATCH, SEQ, D), 1.0)
    d["p"] = nrm((DEPTH, BATCH, SEQ, PLE_DIM), 1.0)
    d["ffn1_pre_g"] = gain((L, D))
    d["ffn1_post_g"] = gain((L, D))
    d["ffn1_w_gate"] = nrm((L, D, F), D ** -0.5)
    d["ffn1_w_up"] = nrm((L, D, F), D ** -0.5)
    d["ffn1_w_down"] = nrm((L, F, D), F ** -0.5)
    d["mix_pre_g"] = gain((L, D))
    d["mix_post_g"] = gain((L, D))
    d["w_in"] = nrm((L, D, IN_WIDTH), D ** -0.5)
    d["cmp_k_pe"] = nrm((L, CMP_BLOCK, NSA_KV_DIM), 0.1)
    d["cmp_k_w1"] = nrm((L, CMP_BLOCK, NSA_KV_DIM, NSA_KV_DIM), (CMP_BLOCK * NSA_KV_DIM) ** -0.5)
    d["cmp_k_w2"] = nrm((L, NSA_KV_DIM, NSA_KV_DIM), NSA_KV_DIM ** -0.5)
    d["cmp_v_pe"] = nrm((L, CMP_BLOCK, NSA_KV_DIM), 0.1)
    d["cmp_v_w1"] = nrm((L, CMP_BLOCK, NSA_KV_DIM, NSA_KV_DIM), (CMP_BLOCK * NSA_KV_DIM) ** -0.5)
    d["cmp_v_w2"] = nrm((L, NSA_KV_DIM, NSA_KV_DIM), NSA_KV_DIM ** -0.5)
    d["nsa_out_g"] = gain((L, NSA_HEADS, NSA_HEAD_DIM))
    d["gla_gate_up"] = nrm((L, GLA_GATE_RANK, GLA_HEADS * GLA_KEY_DIM), GLA_GATE_RANK ** -0.5)
    d["gla_gate_bias"] = nrm((L, GLA_HEADS * GLA_KEY_DIM), 0.1)
    d["gla_out_g"] = gain((L, GLA_HEADS, GLA_VAL_DIM))
    d["w_out"] = nrm((L, MIX_WIDTH, D), MIX_WIDTH ** -0.5)
    d["ffn2_pre_g"] = gain((L, D))
    d["ffn2_post_g"] = gain((L, D))
    d["ffn2_w_gate"] = nrm((L, D, F), D ** -0.5)
    d["ffn2_w_up"] = nrm((L, D, F), D ** -0.5)
    d["ffn2_w_down"] = nrm((L, F, D), F ** -0.5)
    d["ple_proj"] = nrm((L, PLE_DIM, D), PLE_DIM ** -0.5)
    d["ple_gate"] = nrm((L, D, D), D ** -0.5)
    d["ple_post_g"] = gain((L, D))
    return d


def reference(x, p, ffn1_pre_g, ffn1_post_g, ffn1_w_gate, ffn1_w_up, ffn1_w_down,
              mix_pre_g, mix_post_g, w_in, cmp_k_pe, cmp_k_w1, cmp_k_w2,
              cmp_v_pe, cmp_v_w1, cmp_v_w2, nsa_out_g, gla_gate_up, gla_gate_bias,
              gla_out_g, w_out, ffn2_pre_g, ffn2_post_g, ffn2_w_gate, ffn2_w_up,
              ffn2_w_down, ple_proj, ple_gate, ple_post_g):
    B, S, D = x.shape
    split_points = np.cumsum(IN_SPLITS)[:-1].tolist()
    h = x
    for i in range(DEPTH):
        f = _swiglu(_rmsnorm(h, ffn1_pre_g[i]), ffn1_w_gate[i], ffn1_w_up[i], ffn1_w_down[i])
        h = h + 0.5 * _rmsnorm(f, ffn1_post_g[i])

        u = _rmsnorm(h, mix_pre_g[i])
        proj = u @ w_in[i]
        (nq, nkc, nvc, nks, nvs, nkw, nvw, ngate,
         gq, gk, gv, gdown, gog) = jnp.split(proj, split_points, axis=-1)

        o_nsa = _nsa(nq.reshape(B, S, NSA_HEADS, NSA_HEAD_DIM), nkc, nvc, nks, nvs, nkw, nvw,
                     ngate.reshape(B, S, NSA_HEADS, 3),
                     cmp_k_pe[i], cmp_k_w1[i], cmp_k_w2[i], cmp_v_pe[i], cmp_v_w1[i], cmp_v_w2[i])
        o_nsa = _rmsnorm(o_nsa, nsa_out_g[i]).reshape(B, S, NSA_WIDTH)

        g_pre = (gdown @ gla_gate_up[i] + gla_gate_bias[i]).astype(jnp.float32)
        g_log = (jax.nn.log_sigmoid(g_pre) / GLA_GATE_NORM).reshape(B, S, GLA_HEADS, GLA_KEY_DIM)
        o_gla = _gla(gq.reshape(B, S, GLA_HEADS, GLA_KEY_DIM),
                     gk.reshape(B, S, GLA_HEADS, GLA_KEY_DIM),
                     gv.reshape(B, S, GLA_HEADS, GLA_VAL_DIM), g_log)
        o_gla = _rmsnorm(o_gla, gla_out_g[i]) * jax.nn.silu(gog.reshape(B, S, GLA_HEADS, GLA_VAL_DIM))
        o_gla = o_gla.reshape(B, S, GLA_WIDTH)

        mix = jnp.concatenate([o_nsa, o_gla], axis=-1) @ w_out[i]
        h = h + _rmsnorm(mix, mix_post_g[i])

        f = _swiglu(_rmsnorm(h, ffn2_pre_g[i]), ffn2_w_gate[i], ffn2_w_up[i], ffn2_w_down[i])
        h = h + 0.5 * _rmsnorm(f, ffn2_post_g[i])

        e = p[i] @ ple_proj[i]
        gate = jax.nn.sigmoid(h @ ple_gate[i])
        h = h + _rmsnorm(e * gate, ple_post_g[i])
    return h
```

```python
import functools
import math

import numpy as np
import jax
import jax.numpy as jnp
from jax import lax
from jax.experimental import pallas as pl
from jax.experimental.pallas import tpu as pltpu

F32 = jnp.float32
BF16 = jnp.bfloat16

EPS = 1e-6
NSA_HEADS = 4
NSA_HEAD_DIM = 128
CMP_BLOCK = 32
CMP_STRIDE = 16
SLC_BLOCK = 64
SLC_TOPK = 16
WINDOW = 512
FORCE_BONUS = 1e4
NEG = -1e30
GLA_HEADS = 4
GLA_KEY_DIM = 64
GLA_VAL_DIM = 128
GLA_GATE_RANK = 16
GLA_GATE_NORM = 16.0
GLA_CHUNK = 64

LANES = 128
VMEM_LIMIT = 56 * 1024 * 1024

FFN_TM = 1024
FFN_TF = 256
TOK_TM = 512
NSA_TQ = 128
NSA_TK = 512
GLA_TG = 256
WIN_SLAB = WINDOW + NSA_TQ


def _dot(a, b):
    return jnp.dot(a, b, preferred_element_type=F32)


def _dot_nt(a, b):
    return lax.dot_general(a, b, (((1,), (1,)), ((), ())), preferred_element_type=F32)


def _dot_tn(a, b):
    return lax.dot_general(a, b, (((0,), (0,)), ((), ())), preferred_element_type=F32)


def _split(x):
    hi = x.astype(BF16)
    lo = (x - hi.astype(F32)).astype(BF16)
    return hi, lo


def _rms(x, g):
    return x * lax.rsqrt(jnp.mean(x * x, axis=-1, keepdims=True) + EPS) * g


def _sigmoid(x):
    return 1.0 / (1.0 + jnp.exp(-x))


def _cparams(sem):
    return pltpu.CompilerParams(dimension_semantics=sem, vmem_limit_bytes=VMEM_LIMIT)


def _ffn_kernel(h_ref, pre_ref, post_ref, wg_ref, wu_ref, wd_ref, o_ref, u_ref, acc_ref):
    f = pl.program_id(1)

    @pl.when(f == 0)
    def _():
        u_ref[...] = _rms(h_ref[...], pre_ref[...]).astype(BF16)
        acc_ref[...] = jnp.zeros_like(acc_ref)

    u = u_ref[...]
    a = _dot(u, wg_ref[...])
    b = _dot(u, wu_ref[...])
    hid = (a * _sigmoid(a) * b).astype(BF16)
    acc_ref[...] += _dot(hid, wd_ref[...])

    @pl.when(f == pl.num_programs(1) - 1)
    def _():
        o_ref[...] = h_ref[...] + 0.5 * _rms(acc_ref[...], post_ref[...])


def _ffn(h, pre_g, post_g, wg, wu, wd):
    t, d = h.shape
    ff = wg.shape[1]
    tm = min(FFN_TM, t)
    return pl.pallas_call(
        _ffn_kernel,
        grid=(t // tm, ff // FFN_TF),
        in_specs=[
            pl.BlockSpec((tm, d), lambda i, f: (i, 0)),
            pl.BlockSpec((1, d), lambda i, f: (0, 0)),
            pl.BlockSpec((1, d), lambda i, f: (0, 0)),
            pl.BlockSpec((d, FFN_TF), lambda i, f: (0, f)),
            pl.BlockSpec((d, FFN_TF), lambda i, f: (0, f)),
            pl.BlockSpec((FFN_TF, d), lambda i, f: (f, 0)),
        ],
        out_specs=pl.BlockSpec((tm, d), lambda i, f: (i, 0)),
        out_shape=jax.ShapeDtypeStruct((t, d), F32),
        scratch_shapes=[pltpu.VMEM((tm, d), BF16), pltpu.VMEM((tm, d), F32)],
        compiler_params=_cparams(("parallel", "arbitrary")),
        name="ffn",
    )(h, pre_g, post_g, wg, wu, wd)


_P_NQ = (0, 512)
_P_KC = (512, 640)
_P_VC = (640, 768)
_P_KS = (768, 896)
_P_VS = (896, 1024)
_P_KW = (1024, 1152)
_P_VW = (1152, 1280)
_P_MISC = (1280, 1408)
_P_GQ = (1408, 1664)
_P_GK = (1664, 1920)
_P_GV = (1920, 2432)
_P_GOG = (2432, 2944)
_P_WIDTH = 2944
_MISC_GDOWN = 3 * NSA_HEADS


def _proj_kernel(seq, h_ref, g_ref, w_ref, nq_ref, kc_ref, vc_ref, ks_ref, vs_ref, kw_ref, vw_ref,
                 misc_ref, gq_ref, gk_ref, gv_ref, gog_ref):
    tm = h_ref.shape[0]
    u = _rms(h_ref[...], g_ref[...]).astype(BF16)

    def grp(lohi):
        return _dot(u, w_ref[:, lohi[0]:lohi[1]])

    nq_ref[...] = (grp(_P_NQ) * (NSA_HEAD_DIM ** -0.5)).astype(BF16)
    kc_ref[...] = grp(_P_KC).astype(BF16)
    vc_ref[...] = grp(_P_VC).astype(BF16)
    base = (pl.program_id(0) % (seq // tm)) * tm
    row = lax.broadcasted_iota(jnp.int32, (tm, LANES), 0)
    lane = lax.broadcasted_iota(jnp.int32, (tm, LANES), 1)
    onehot = jnp.where(lane == ((base + row) >> int(math.log2(SLC_BLOCK))), 1.0, 0.0).astype(BF16)
    ks_ref[:, 0:LANES] = grp(_P_KS).astype(BF16)
    ks_ref[:, LANES:2 * LANES] = onehot
    vs_ref[...] = grp(_P_VS).astype(BF16)
    kw_ref[...] = grp(_P_KW).astype(BF16)
    vw_ref[...] = grp(_P_VW).astype(BF16)
    misc_ref[...] = grp(_P_MISC)
    gq_ref[...] = grp(_P_GQ).astype(BF16)
    gk_ref[...] = grp(_P_GK).astype(BF16)
    gv_ref[...] = grp(_P_GV).astype(BF16)
    gog_ref[...] = grp(_P_GOG).astype(BF16)


def _proj(h, g, w_all, seq):
    t, d = h.shape
    tm = min(TOK_TM, seq)
    widths = [(512, BF16), (128, BF16), (128, BF16), (256, BF16), (128, BF16), (128, BF16), (128, BF16),
              (128, F32), (256, BF16), (256, BF16), (512, BF16), (512, BF16)]
    return pl.pallas_call(
        functools.partial(_proj_kernel, seq),
        grid=(t // tm,),
        in_specs=[
            pl.BlockSpec((tm, d), lambda i: (i, 0)),
            pl.BlockSpec((1, d), lambda i: (0, 0)),
            pl.BlockSpec((d, _P_WIDTH), lambda i: (0, 0)),
        ],
        out_specs=[pl.BlockSpec((tm, w), lambda i: (i, 0)) for w, _ in widths],
        out_shape=[jax.ShapeDtypeStruct((t, w), dt) for w, dt in widths],
        compiler_params=_cparams(("parallel",)),
        name="proj",
    )(h, g, w_all)


def _compress_kernel(k_ref, v_ref, pek_ref, w1k_ref, w2k_ref, pev_ref, w1v_ref, w2v_ref, kc_ref, vc_ref):
    half = CMP_STRIDE * NSA_HEAD_DIM

    def one(x_ref, pe_ref, w1_ref, w2_ref, o_ref):
        x = x_ref[...].astype(F32)
        n = x.shape[0]
        a = _dot((x + pe_ref[0:1, :]).astype(BF16), w1_ref[0:half, :])
        b = _dot((x + pe_ref[1:2, :]).astype(BF16), w1_ref[half:2 * half, :])
        hid = a + pltpu.roll(b, n - 1, 0)
        act = (hid * _sigmoid(hid)).astype(BF16)
        o_ref[...] = _dot(act, w2_ref[...]).astype(BF16)

    one(k_ref, pek_ref, w1k_ref, w2k_ref, kc_ref)
    one(v_ref, pev_ref, w1v_ref, w2v_ref, vc_ref)


def _compress(k2, v2, pek, w1k, w2k, pev, w1v, w2v):
    b, n, w = k2.shape
    dk = w2k.shape[1]
    big = pl.BlockSpec((None, n, w), lambda i: (i, 0, 0))

    def full(a):
        return pl.BlockSpec(a.shape, lambda i: (0,) * a.ndim)

    return pl.pallas_call(
        _compress_kernel,
        grid=(b,),
        in_specs=[big, big, full(pek), full(w1k), full(w2k), full(pev), full(w1v), full(w2v)],
        out_specs=[pl.BlockSpec((None, n, dk), lambda i: (i, 0, 0))] * 2,
        out_shape=[jax.ShapeDtypeStruct((b, n, dk), BF16)] * 2,
        compiler_params=_cparams(("parallel",)),
        name="compress",
    )(k2, v2, pek, w1k, w2k, pev, w1v, w2v)


def _nsa_kernel(q_ref, kc_ref, vc_ref, ks_ref, vs_ref, kw_ref, vw_ref, misc_ref, ov_ref, og_ref, o_ref):
    tq = q_ref.shape[0]
    rows = NSA_HEADS * tq
    q0 = pl.program_id(1) * tq
    q = q_ref[...]
    qh = jnp.concatenate([q[:, h * NSA_HEAD_DIM:(h + 1) * NSA_HEAD_DIM] for h in range(NSA_HEADS)], axis=0)
    t_row = q0 + (lax.broadcasted_iota(jnp.int32, (rows, 1), 0) & (tq - 1))

    n_c = kc_ref.shape[0]
    s = _dot_nt(qh, kc_ref[...])
    col = lax.broadcasted_iota(jnp.int32, (rows, n_c), 1)
    m_c = (col * CMP_STRIDE + (CMP_BLOCK - 1)) <= t_row
    mx = jnp.max(jnp.where(m_c, s, NEG), axis=1, keepdims=True)
    e = jnp.where(m_c, jnp.exp(s - mx), 0.0)
    l = jnp.sum(e, axis=1, keepdims=True)
    p_c = e * (1.0 / jnp.where(l > 0.0, l, 1.0))
    o_c = _dot(p_c.astype(BF16), vc_ref[...])

    p_sum = p_c[0:tq]
    for h in range(1, NSA_HEADS):
        p_sum = p_sum + p_c[h * tq:(h + 1) * tq]
    p_hi, p_lo = _split(p_sum)
    imp = _dot(p_hi, ov_ref[...]) + _dot(p_lo, ov_ref[...])
    blk = lax.broadcasted_iota(jnp.int32, (tq, LANES), 1)
    cur = (q0 + lax.broadcasted_iota(jnp.int32, (tq, LANES), 0)) >> int(math.log2(SLC_BLOCK))
    forced = (blk == 0) | (blk == cur) | (blk == cur - 1)
    val = jnp.where(blk <= cur, imp + jnp.where(forced, FORCE_BONUS, 0.0), -1.0)
    blk_f = blk.astype(F32)
    bias = jnp.full((tq, LANES), NEG, F32)
    for _ in range(SLC_TOPK):
        top = jnp.max(val, axis=1, keepdims=True)
        first = jnp.min(jnp.where(val == top, blk_f, float(LANES)), axis=1, keepdims=True)
        pick = blk_f == first
        bias = jnp.where(pick, 0.0, bias)
        val = jnp.where(pick, -2.0, val)
    bias = bias.astype(BF16)
    q_aug = jnp.concatenate([qh, jnp.concatenate([bias] * NSA_HEADS, axis=0)], axis=1)

    def sel_tile(j, carry, causal):
        m_i, l_i, acc = carry
        k0 = pl.multiple_of(j * NSA_TK, NSA_TK)
        sc = _dot_nt(q_aug, ks_ref[pl.ds(k0, NSA_TK), :])
        if causal:
            pos = k0 + lax.broadcasted_iota(jnp.int32, (rows, NSA_TK), 1)
            sc = jnp.where(pos <= t_row, sc, NEG)
        m_n = jnp.maximum(m_i, jnp.max(sc, axis=1, keepdims=True))
        alpha = jnp.exp(m_i - m_n)
        p = jnp.exp(sc - m_n)
        l_n = alpha * l_i + jnp.sum(p, axis=1, keepdims=True)
        acc_n = alpha * acc + _dot(p.astype(BF16), vs_ref[pl.ds(k0, NSA_TK), :])
        return m_n, l_n, acc_n

    jd = q0 // NSA_TK
    init = (jnp.full((rows, 1), NEG, F32), jnp.zeros((rows, 1), F32), jnp.zeros((rows, NSA_HEAD_DIM), F32))
    carry = lax.fori_loop(0, jd, lambda j, c: sel_tile(j, c, False), init)
    _, l_s, acc_s = sel_tile(jd, carry, True)
    o_s = acc_s * (1.0 / l_s)

    w0 = pl.multiple_of(jnp.maximum(q0 - WINDOW, 0), NSA_TQ)
    sw = _dot_nt(qh, kw_ref[pl.ds(w0, WIN_SLAB), :])
    diff = t_row - (w0 + lax.broadcasted_iota(jnp.int32, (rows, WIN_SLAB), 1))
    m_w = (diff >= 0) & (diff < WINDOW)
    sw = jnp.where(m_w, sw, NEG)
    ew = jnp.exp(sw - jnp.max(sw, axis=1, keepdims=True))
    o_w = _dot(ew.astype(BF16), vw_ref[pl.ds(w0, WIN_SLAB), :]) * (1.0 / jnp.sum(ew, axis=1, keepdims=True))

    gates = _sigmoid(misc_ref[...])
    outs = []
    for h in range(NSA_HEADS):
        r = slice(h * tq, (h + 1) * tq)
        o_h = (gates[:, 3 * h:3 * h + 1] * o_c[r] + gates[:, 3 * h + 1:3 * h + 2] * o_s[r]
               + gates[:, 3 * h + 2:3 * h + 3] * o_w[r])
        outs.append(_rms(o_h, og_ref[:, h * NSA_HEAD_DIM:(h + 1) * NSA_HEAD_DIM]))
    o_ref[...] = jnp.concatenate(outs, axis=1).astype(BF16)


def _nsa(nq, kc, vc, ks, vs, kw, vw, misc, ov, og):
    b, s, wq = nq.shape
    tq = NSA_TQ

    def per_batch(a):
        return pl.BlockSpec((None,) + a.shape[1:], lambda i, j: (i, 0, 0))

    def per_tile(a):
        return pl.BlockSpec((None, tq, a.shape[2]), lambda i, j: (i, j, 0))

    def full(a):
        return pl.BlockSpec(a.shape, lambda i, j: (0, 0))

    return pl.pallas_call(
        _nsa_kernel,
        grid=(b, s // tq),
        in_specs=[per_tile(nq), per_batch(kc), per_batch(vc), per_batch(ks), per_batch(vs), per_batch(kw),
                  per_batch(vw), per_tile(misc), full(ov), full(og)],
        out_specs=pl.BlockSpec((None, tq, wq), lambda i, j: (i, j, 0)),
        out_shape=jax.ShapeDtypeStruct((b, s, wq), BF16),
        compiler_params=_cparams(("parallel", "arbitrary")),
        name="nsa",
    )(nq, kc, vc, ks, vs, kw, vw, misc, ov, og)


def _gla_kernel(q_ref, k_ref, v_ref, og_ref, misc_ref, up_ref, bias_ref, g_ref, o_ref, st_ref):
    c = GLA_CHUNK
    kw = GLA_HEADS * GLA_KEY_DIM

    @pl.when(pl.program_id(1) == 0)
    def _():
        st_ref[...] = jnp.zeros_like(st_ref)

    m_hi, m_lo = _split(misc_ref[...])
    u_hi, u_lo = _split(up_ref[...])
    g_pre = _dot(m_hi, u_hi) + _dot(m_lo, u_hi) + _dot(m_hi, u_lo) + bias_ref[...]
    g_log = (jnp.minimum(g_pre, 0.0) - jnp.log(1.0 + jnp.exp(-jnp.abs(g_pre)))) * (1.0 / GLA_GATE_NORM)

    ri = lax.broadcasted_iota(jnp.int32, (c, c), 0)
    ci = lax.broadcasted_iota(jnp.int32, (c, c), 1)
    tri = jnp.where(ci <= ri, 1.0, 0.0).astype(BF16)
    causal = jnp.concatenate([jnp.where(ci <= ri, 1.0, 0.0)] * GLA_HEADS, axis=0)
    lane_head = lax.broadcasted_iota(jnp.int32, (c, kw), 1) >> int(math.log2(GLA_KEY_DIM))
    head_mask = [lane_head == h for h in range(GLA_HEADS)]

    def stack_heads(x):
        return jnp.concatenate([jnp.where(head_mask[h], x, 0.0) for h in range(GLA_HEADS)], axis=0)

    st = st_ref[...]
    for cc in range(q_ref.shape[0] // c):
        r = slice(cc * c, (cc + 1) * c)
        g_hi, g_lo = _split(g_log[r])
        bcum = _dot(tri, g_hi) + _dot(tri, g_lo)
        b_last = bcum[c - 1:c, :]
        qf = q_ref[r, :].astype(F32)
        kf = k_ref[r, :].astype(F32)
        v = v_ref[r, :]
        q_dec = qf * jnp.exp(bcum) * (GLA_KEY_DIM ** -0.5)
        k_dec = (kf * jnp.exp(-bcum)).astype(BF16)
        k_end = kf * jnp.exp(b_last - bcum)
        qm = stack_heads(q_dec).astype(BF16)
        a = (_dot_nt(qm, k_dec) * causal).astype(BF16)
        intra = _dot(a, v)
        inter = _dot_nt(qm, st.astype(BF16))
        km = stack_heads(k_end).astype(BF16)
        vst = jnp.concatenate([v[:, h * GLA_VAL_DIM:(h + 1) * GLA_VAL_DIM] for h in range(GLA_HEADS)], axis=0)
        st = st * jnp.exp(b_last) + _dot_tn(vst, km)
        outs = []
        for h in range(GLA_HEADS):
            rr = slice(h * c, (h + 1) * c)
            vs = slice(h * GLA_VAL_DIM, (h + 1) * GLA_VAL_DIM)
            o_h = _rms(intra[rr, vs] + inter[rr], g_ref[:, vs])
            gate = og_ref[r, vs].astype(F32)
            outs.append(o_h * (gate * _sigmoid(gate)))
        o_ref[r, :] = jnp.concatenate(outs, axis=1).astype(BF16)
    st_ref[...] = st


def _gla(gq, gk, gv, gog, misc, up_pad, bias, g):
    b, s, _ = gq.shape
    tg = min(GLA_TG, s)

    def per_tile(a):
        return pl.BlockSpec((None, tg, a.shape[2]), lambda i, j: (i, j, 0))

    def full(a):
        return pl.BlockSpec(a.shape, lambda i, j: (0, 0))

    return pl.pallas_call(
        _gla_kernel,
        grid=(b, s // tg),
        in_specs=[per_tile(gq), per_tile(gk), per_tile(gv), per_tile(gog), per_tile(misc),
                  full(up_pad), full(bias), full(g)],
        out_specs=per_tile(gv),
        out_shape=jax.ShapeDtypeStruct(gv.shape, BF16),
        scratch_shapes=[pltpu.VMEM((GLA_VAL_DIM, GLA_HEADS * GLA_KEY_DIM), F32)],
        compiler_params=_cparams(("parallel", "arbitrary")),
        name="gla",
    )(gq, gk, gv, gog, misc, up_pad, bias, g)


def _mixout_kernel(on_ref, ogl_ref, h_ref, wa_ref, wb_ref, g_ref, o_ref):
    mix = _dot(on_ref[...], wa_ref[...]) + _dot(ogl_ref[...], wb_ref[...])
    o_ref[...] = h_ref[...] + _rms(mix, g_ref[...])


def _mixout(o_nsa, o_gla, h, wa, wb, g):
    t, d = h.shape
    tm = min(TOK_TM, t)
    w = o_nsa.shape[1]
    return pl.pallas_call(
        _mixout_kernel,
        grid=(t // tm,),
        in_specs=[
            pl.BlockSpec((tm, w), lambda i: (i, 0)),
            pl.BlockSpec((tm, w), lambda i: (i, 0)),
            pl.BlockSpec((tm, d), lambda i: (i, 0)),
            pl.BlockSpec(wa.shape, lambda i: (0, 0)),
            pl.BlockSpec(wb.shape, lambda i: (0, 0)),
            pl.BlockSpec((1, d), lambda i: (0, 0)),
        ],
        out_specs=pl.BlockSpec((tm, d), lambda i: (i, 0)),
        out_shape=jax.ShapeDtypeStruct((t, d), F32),
        compiler_params=_cparams(("parallel",)),
        name="mixout",
    )(o_nsa, o_gla, h, wa, wb, g)


def _ple_kernel(h_ref, p_ref, wp_ref, wg_ref, g_ref, o_ref):
    h = h_ref[...]
    e = _dot(p_ref[...].astype(BF16), wp_ref[...])
    gate = _sigmoid(_dot(h.astype(BF16), wg_ref[...]))
    o_ref[...] = h + _rms(e * gate, g_ref[...])


def _ple(h, p, wp, wg, g):
    t, d = h.shape
    tm = min(TOK_TM, t)
    return pl.pallas_call(
        _ple_kernel,
        grid=(t // tm,),
        in_specs=[
            pl.BlockSpec((tm, d), lambda i: (i, 0)),
            pl.BlockSpec((tm, p.shape[1]), lambda i: (i, 0)),
            pl.BlockSpec(wp.shape, lambda i: (0, 0)),
            pl.BlockSpec(wg.shape, lambda i: (0, 0)),
            pl.BlockSpec((1, d), lambda i: (0, 0)),
        ],
        out_specs=pl.BlockSpec((tm, d), lambda i: (i, 0)),
        out_shape=jax.ShapeDtypeStruct((t, d), F32),
        compiler_params=_cparams(("parallel",)),
        name="ple",
    )(h, p, wp, wg, g)


def _overlap_matrix(n_rows):
    n = np.arange(n_rows)[:, None] * CMP_STRIDE
    s = np.arange(LANES)[None, :] * SLC_BLOCK
    return ((n < s + SLC_BLOCK) & (n + CMP_BLOCK > s)).astype(np.float32)


def _layer(h, p, seq, batch, ffn1_pre_g, ffn1_post_g, ffn1_w_gate, ffn1_w_up, ffn1_w_down, mix_pre_g, mix_post_g,
           w_in, cmp_k_pe, cmp_k_w1, cmp_k_w2, cmp_v_pe, cmp_v_w1, cmp_v_w2, nsa_out_g, gla_gate_up, gla_gate_bias,
           gla_out_g, w_out, ffn2_pre_g, ffn2_post_g, ffn2_w_gate, ffn2_w_up, ffn2_w_down, ple_proj, ple_gate,
           ple_post_g):
    d = h.shape[1]
    row = lambda a: a.reshape(1, -1).astype(F32)
    bf = lambda a: a.astype(BF16)

    h1 = _ffn(h, row(ffn1_pre_g), row(ffn1_post_g), bf(ffn1_w_gate), bf(ffn1_w_up), bf(ffn1_w_down))

    o_gate = 512 + 6 * 128
    o_gq = o_gate + 3 * NSA_HEADS
    o_gk = o_gq + GLA_HEADS * GLA_KEY_DIM
    o_gv = o_gk + GLA_HEADS * GLA_KEY_DIM
    o_gd = o_gv + GLA_HEADS * GLA_VAL_DIM
    o_og = o_gd + GLA_GATE_RANK
    pad = jnp.zeros((d, LANES - 3 * NSA_HEADS - GLA_GATE_RANK), w_in.dtype)
    w_all = bf(jnp.concatenate([w_in[:, :o_gate], w_in[:, o_gate:o_gq], w_in[:, o_gd:o_og], pad,
                                w_in[:, o_gq:o_gd], w_in[:, o_og:]], axis=1))
    nq, kc_in, vc_in, ks, vs, kw, vw, misc, gq, gk, gv, gog = _proj(h1, row(mix_pre_g), w_all, seq)

    n_grp = seq // CMP_STRIDE
    grp_w = CMP_STRIDE * NSA_HEAD_DIM
    kc, vc = _compress(
        kc_in.reshape(batch, n_grp, grp_w), vc_in.reshape(batch, n_grp, grp_w),
        cmp_k_pe.reshape(2, grp_w).astype(F32), bf(cmp_k_w1.reshape(2 * grp_w, -1)), bf(cmp_k_w2),
        cmp_v_pe.reshape(2, grp_w).astype(F32), bf(cmp_v_w1.reshape(2 * grp_w, -1)), bf(cmp_v_w2))

    b3 = lambda a: a.reshape(batch, seq, a.shape[-1])
    ov = jnp.asarray(_overlap_matrix(n_grp), BF16)
    o_nsa = _nsa(b3(nq), kc, vc, b3(ks), b3(vs), b3(kw), b3(vw), b3(misc), ov, row(nsa_out_g))

    up_pad = jnp.zeros((LANES, GLA_HEADS * GLA_KEY_DIM), F32)
    up_pad = up_pad.at[_MISC_GDOWN:_MISC_GDOWN + GLA_GATE_RANK].set(gla_gate_up.astype(F32))
    o_gla = _gla(b3(gq), b3(gk), b3(gv), b3(gog), b3(misc), up_pad, row(gla_gate_bias), row(gla_out_g))

    w_o = bf(w_out)
    half = NSA_HEADS * NSA_HEAD_DIM
    h2 = _mixout(o_nsa.reshape(-1, half), o_gla.reshape(-1, half), h1, w_o[:half], w_o[half:], row(mix_post_g))
    h3 = _ffn(h2, row(ffn2_pre_g), row(ffn2_post_g), bf(ffn2_w_gate), bf(ffn2_w_up), bf(ffn2_w_down))
    return _ple(h3, p, bf(ple_proj), bf(ple_gate), row(ple_post_g))


def kernel(x, p, ffn1_pre_g, ffn1_post_g, ffn1_w_gate, ffn1_w_up, ffn1_w_down, mix_pre_g, mix_post_g, w_in, cmp_k_pe, cmp_k_w1, cmp_k_w2, cmp_v_pe, cmp_v_w1, cmp_v_w2, nsa_out_g, gla_gate_up, gla_gate_bias, gla_out_g, w_out, ffn2_pre_g, ffn2_post_g, ffn2_w_gate, ffn2_w_up, ffn2_w_down, ple_proj, ple_gate, ple_post_g):
    batch, seq, d = x.shape
    depth = p.shape[0]
    assert seq % NSA_TK == 0 and seq >= WIN_SLAB and seq // SLC_BLOCK <= LANES
    h = x.reshape(batch * seq, d)
    params = (ffn1_pre_g, ffn1_post_g, ffn1_w_gate, ffn1_w_up, ffn1_w_down, mix_pre_g, mix_post_g, w_in, cmp_k_pe,
              cmp_k_w1, cmp_k_w2, cmp_v_pe, cmp_v_w1, cmp_v_w2, nsa_out_g, gla_gate_up, gla_gate_bias, gla_out_g,
              w_out, ffn2_pre_g, ffn2_post_g, ffn2_w_gate, ffn2_w_up, ffn2_w_down, ple_proj, ple_gate, ple_post_g)
    for i in range(depth):
        h = _layer(h, p[i].reshape(batch * seq, -1), seq, batch, *[a[i] for a in params])
    return h.reshape(batch, seq, d)
```

```python
import functools
import math

import numpy as np
import jax
import jax.numpy as jnp
from jax import lax
from jax.experimental import pallas as pl
from jax.experimental.pallas import tpu as pltpu

F32 = jnp.float32
BF16 = jnp.bfloat16

EPS = 1e-6
NSA_HEADS = 4
NSA_HEAD_DIM = 128
CMP_BLOCK = 32
CMP_STRIDE = 16
SLC_BLOCK = 64
SLC_TOPK = 16
WINDOW = 512
FORCE_BONUS = 1e4
NEG = -1e30
GLA_HEADS = 4
GLA_KEY_DIM = 64
GLA_VAL_DIM = 128
GLA_GATE_RANK = 16
GLA_GATE_NORM = 16.0
GLA_CHUNK = 64

LOG2E = 1.4426950408889634
N_FORCED = 3

LANES = 128
VMEM_LIMIT = 56 * 1024 * 1024

FFN_TM = 1024
FFN_TF = 256
TOK_TM = 512
NSA_TQ = 128
NSA_SUB = 4
NSA_TK = 512
GLA_TG = 256
GLA_NB = 2
WIN_SLAB = WINDOW + NSA_TQ


def _dot(a, b):
    return jnp.dot(a, b, preferred_element_type=F32)


def _dot_nt(a, b):
    return lax.dot_general(a, b, (((1,), (1,)), ((), ())), preferred_element_type=F32)


def _dot_tn(a, b):
    return lax.dot_general(a, b, (((0,), (0,)), ((), ())), preferred_element_type=F32)


def _split(x):
    hi = x.astype(BF16)
    lo = (x - hi.astype(F32)).astype(BF16)
    return hi, lo


def _rms(x, g):
    return x * lax.rsqrt(jnp.mean(x * x, axis=-1, keepdims=True) + EPS) * g


def _sigmoid(x):
    return 1.0 / (1.0 + jnp.exp(-x))


def _cparams(sem):
    return pltpu.CompilerParams(dimension_semantics=sem, vmem_limit_bytes=VMEM_LIMIT)


def _ffn_kernel(h_ref, pre_ref, post_ref, wg_ref, wu_ref, wd_ref, o_ref, u_ref, acc_ref):
    f = pl.program_id(1)

    @pl.when(f == 0)
    def _():
        u_ref[...] = _rms(h_ref[...], pre_ref[...]).astype(BF16)
        acc_ref[...] = jnp.zeros_like(acc_ref)

    u = u_ref[...]
    a = _dot(u, wg_ref[...])
    b = _dot(u, wu_ref[...])
    hid = (a * _sigmoid(a) * b).astype(BF16)
    acc_ref[...] += _dot(hid, wd_ref[...])

    @pl.when(f == pl.num_programs(1) - 1)
    def _():
        o_ref[...] = h_ref[...] + 0.5 * _rms(acc_ref[...], post_ref[...])


def _ffn(h, pre_g, post_g, wg, wu, wd):
    t, d = h.shape
    ff = wg.shape[1]
    tm = min(FFN_TM, t)
    return pl.pallas_call(
        _ffn_kernel,
        grid=(t // tm, ff // FFN_TF),
        in_specs=[
            pl.BlockSpec((tm, d), lambda i, f: (i, 0)),
            pl.BlockSpec((1, d), lambda i, f: (0, 0)),
            pl.BlockSpec((1, d), lambda i, f: (0, 0)),
            pl.BlockSpec((d, FFN_TF), lambda i, f: (0, f)),
            pl.BlockSpec((d, FFN_TF), lambda i, f: (0, f)),
            pl.BlockSpec((FFN_TF, d), lambda i, f: (f, 0)),
        ],
        out_specs=pl.BlockSpec((tm, d), lambda i, f: (i, 0)),
        out_shape=jax.ShapeDtypeStruct((t, d), F32),
        scratch_shapes=[pltpu.VMEM((tm, d), BF16), pltpu.VMEM((tm, d), F32)],
        compiler_params=_cparams(("parallel", "arbitrary")),
        name="ffn",
    )(h, pre_g, post_g, wg, wu, wd)


_P_NQ = (0, 512)
_P_KC = (512, 640)
_P_VC = (640, 768)
_P_KS = (768, 896)
_P_VS = (896, 1024)
_P_KW = (1024, 1152)
_P_VW = (1152, 1280)
_P_MISC = (1280, 1408)
_P_GQ = (1408, 1664)
_P_GK = (1664, 1920)
_P_GV = (1920, 2432)
_P_GOG = (2432, 2944)
_P_WIDTH = 2944
_MISC_GDOWN = 3 * NSA_HEADS


def _proj_kernel(seq, h_ref, g_ref, w_ref, nq_ref, kc_ref, vc_ref, ks_ref, vs_ref, kw_ref, vw_ref,
                 misc_ref, gq_ref, gk_ref, gv_ref, gog_ref):
    tm = h_ref.shape[0]
    u = _rms(h_ref[...], g_ref[...]).astype(BF16)

    def grp(lohi):
        return _dot(u, w_ref[:, lohi[0]:lohi[1]])

    nq_ref[...] = (grp(_P_NQ) * (NSA_HEAD_DIM ** -0.5 * LOG2E)).astype(BF16)
    kc_ref[...] = grp(_P_KC).astype(BF16)
    vc_ref[...] = grp(_P_VC).astype(BF16)
    base = (pl.program_id(0) % (seq // tm)) * tm
    row = lax.broadcasted_iota(jnp.int32, (tm, LANES), 0)
    lane = lax.broadcasted_iota(jnp.int32, (tm, LANES), 1)
    onehot = jnp.where(lane == ((base + row) >> int(math.log2(SLC_BLOCK))), 1.0, 0.0).astype(BF16)
    ks_ref[:, 0:LANES] = grp(_P_KS).astype(BF16)
    ks_ref[:, LANES:2 * LANES] = onehot
    vs_ref[...] = grp(_P_VS).astype(BF16)
    kw_ref[...] = grp(_P_KW).astype(BF16)
    vw_ref[...] = grp(_P_VW).astype(BF16)
    misc_ref[...] = grp(_P_MISC)
    gq_ref[...] = grp(_P_GQ).astype(BF16)
    gk_ref[...] = grp(_P_GK).astype(BF16)
    gv_ref[...] = grp(_P_GV).astype(BF16)
    gog_ref[...] = grp(_P_GOG).astype(BF16)


def _proj(h, g, w_all, seq):
    t, d = h.shape
    tm = min(TOK_TM, seq)
    widths = [(512, BF16), (128, BF16), (128, BF16), (256, BF16), (128, BF16), (128, BF16), (128, BF16),
              (128, F32), (256, BF16), (256, BF16), (512, BF16), (512, BF16)]
    return pl.pallas_call(
        functools.partial(_proj_kernel, seq),
        grid=(t // tm,),
        in_specs=[
            pl.BlockSpec((tm, d), lambda i: (i, 0)),
            pl.BlockSpec((1, d), lambda i: (0, 0)),
            pl.BlockSpec((d, _P_WIDTH), lambda i: (0, 0)),
        ],
        out_specs=[pl.BlockSpec((tm, w), lambda i: (i, 0)) for w, _ in widths],
        out_shape=[jax.ShapeDtypeStruct((t, w), dt) for w, dt in widths],
        compiler_params=_cparams(("parallel",)),
        name="proj",
    )(h, g, w_all)


def _compress_kernel(k_ref, v_ref, pek_ref, w1k_ref, w2k_ref, pev_ref, w1v_ref, w2v_ref, kc_ref, vc_ref):
    half = CMP_STRIDE * NSA_HEAD_DIM

    def one(x_ref, pe_ref, w1_ref, w2_ref):
        x = x_ref[...].astype(F32)
        n = x.shape[0]
        a = _dot((x + pe_ref[0:1, :]).astype(BF16), w1_ref[0:half, :])
        b = _dot((x + pe_ref[1:2, :]).astype(BF16), w1_ref[half:2 * half, :])
        hid = a + pltpu.roll(b, n - 1, 0)
        act = (hid * _sigmoid(hid)).astype(BF16)
        return _dot(act, w2_ref[...])

    kc_ref[...] = one(k_ref, pek_ref, w1k_ref, w2k_ref).astype(BF16)
    vc_ref[...] = one(v_ref, pev_ref, w1v_ref, w2v_ref).T.astype(BF16)


def _compress(k2, v2, pek, w1k, w2k, pev, w1v, w2v):
    b, n, w = k2.shape
    dk = w2k.shape[1]
    big = pl.BlockSpec((None, n, w), lambda i: (i, 0, 0))

    def full(a):
        return pl.BlockSpec(a.shape, lambda i: (0,) * a.ndim)

    return pl.pallas_call(
        _compress_kernel,
        grid=(b,),
        in_specs=[big, big, full(pek), full(w1k), full(w2k), full(pev), full(w1v), full(w2v)],
        out_specs=[pl.BlockSpec((None, n, dk), lambda i: (i, 0, 0)), pl.BlockSpec((None, dk, n), lambda i: (i, 0, 0))],
        out_shape=[jax.ShapeDtypeStruct((b, n, dk), BF16), jax.ShapeDtypeStruct((b, dk, n), BF16)],
        compiler_params=_cparams(("parallel",)),
        name="compress",
    )(k2, v2, pek, w1k, w2k, pev, w1v, w2v)


def _nsa_kernel(q_ref, kc_ref, vct_ref, ks_ref, vs_ref, kw_ref, vw_ref, misc_ref, ovt_ref, og_ref, o_ref,
                qaug_ref, m_ref, l_ref, acc_ref, oc_ref, ow_ref):
    tq = NSA_TQ
    rows = NSA_HEADS * tq
    jd = pl.program_id(1)
    step0 = jd * NSA_TK
    n_c = kc_ref.shape[0]
    log_g = int(math.log2(CMP_STRIDE))
    row_i = lax.broadcasted_iota(jnp.int32, (rows, LANES), 0) & (tq - 1)
    lane = lax.broadcasted_iota(jnp.int32, (rows, LANES), 1)
    lane_q = lax.broadcasted_iota(jnp.int32, (1, rows), 1) & (tq - 1)
    tri_diag = jnp.where(lane <= row_i, 0.0, NEG)
    tri_far = jnp.where(lane > row_i, 0.0, NEG)
    q_grp = jnp.where(lane == ((row_i + 1) >> log_g), 1.0, 0.0).astype(BF16)
    n_minus_g = (lax.broadcasted_iota(jnp.int32, (n_c, LANES), 0)
                 - lax.broadcasted_iota(jnp.int32, (n_c, LANES), 1))

    def heads_to_rows(q):
        return jnp.concatenate([q[:, h * NSA_HEAD_DIM:(h + 1) * NSA_HEAD_DIM] for h in range(NSA_HEADS)], axis=0)

    for r in range(NSA_SUB):
        q0 = step0 + r * tq
        qh = heads_to_rows(q_ref[r * tq:(r + 1) * tq, :])
        vis = jnp.where(n_minus_g <= (q0 >> log_g) - 2, 0.0, NEG).astype(BF16)
        st = _dot_nt(jnp.concatenate([kc_ref[...], vis], axis=1), jnp.concatenate([qh, q_grp], axis=1))
        et = jnp.exp2(st - jnp.max(st, axis=0, keepdims=True))
        inv = jnp.where(q0 + lane_q >= CMP_BLOCK - 1, 1.0 / jnp.sum(et, axis=0, keepdims=True), 0.0)
        p_t = et * inv
        oc_ref[r] = _dot(vct_ref[...], p_t.astype(BF16)).T
        p_sum = p_t[:, 0:tq]
        for h in range(1, NSA_HEADS):
            p_sum = p_sum + p_t[:, h * tq:(h + 1) * tq]
        p_hi, p_lo = _split(p_sum)
        imp = _dot(ovt_ref[...], p_hi) + _dot(ovt_ref[...], p_lo)
        blk = lax.broadcasted_iota(jnp.int32, (LANES, tq), 0)
        cur = (q0 + lax.broadcasted_iota(jnp.int32, (LANES, tq), 1)) >> int(math.log2(SLC_BLOCK))
        forced = (blk == 0) | (blk == cur) | (blk == cur - 1)
        val = jnp.where(forced | (blk > cur), -1.0, imp)
        blk_f = blk.astype(F32)
        bias_t = jnp.where(forced, 0.0, NEG)
        for _ in range(SLC_TOPK - N_FORCED):
            top = jnp.max(val, axis=0, keepdims=True)
            first = jnp.min(jnp.where(val == top, blk_f, float(LANES)), axis=0, keepdims=True)
            pick = blk_f == first
            bias_t = jnp.where(pick, 0.0, bias_t)
            val = jnp.where(pick, -2.0, val)
        bias = bias_t.T.astype(BF16)
        qaug_ref[r] = jnp.concatenate([qh, jnp.concatenate([bias] * NSA_HEADS, axis=0)], axis=1)
        m_ref[r] = jnp.full((rows, LANES), NEG, F32)
        l_ref[r] = jnp.zeros((rows, LANES), F32)
        acc_ref[r] = jnp.zeros((rows, NSA_HEAD_DIM), F32)

    def mask_last_chunk(sc):
        w = sc.shape[1]
        if w == LANES:
            return sc + tri_diag
        return jnp.concatenate([sc[:, :w - LANES], sc[:, w - LANES:] + tri_diag], axis=1)

    def flash(r, kt, vt, diag):
        sc = _dot_nt(qaug_ref[r], kt)
        if diag:
            sc = mask_last_chunk(sc)
        m_i = m_ref[r]
        m_n = jnp.maximum(m_i, jnp.max(sc, axis=1, keepdims=True))
        alpha = jnp.exp2(m_i - m_n)
        p = jnp.exp2(sc - jnp.concatenate([m_n] * (sc.shape[1] // LANES), axis=1))
        l_ref[r] = alpha * l_ref[r] + jnp.sum(p, axis=1, keepdims=True)
        acc_ref[r] = alpha * acc_ref[r] + _dot(p.astype(BF16), vt)
        m_ref[r] = m_n

    def full_tile(j, carry):
        k0 = pl.multiple_of(j * NSA_TK, NSA_TK)
        kt = ks_ref[pl.ds(k0, NSA_TK), :]
        vt = vs_ref[pl.ds(k0, NSA_TK), :]
        for r in range(NSA_SUB):
            flash(r, kt, vt, False)
        return carry

    lax.fori_loop(0, jd, full_tile, 0)
    d0 = pl.multiple_of(step0, NSA_TK)
    for r in range(NSA_SUB):
        w = (r + 1) * tq
        flash(r, ks_ref[pl.ds(d0, w), :], vs_ref[pl.ds(d0, w), :], True)

    def window(r, sw, vw):
        ew = jnp.exp2(sw - jnp.max(sw, axis=1, keepdims=True))
        ow_ref[r] = _dot(ew.astype(BF16), vw) * (1.0 / jnp.sum(ew, axis=1, keepdims=True))

    @pl.when(jd == 0)
    def _():
        for r in range(NSA_SUB):
            w = (r + 1) * tq
            sw = _dot_nt(qaug_ref[r][:, 0:NSA_HEAD_DIM], kw_ref[0:w, :])
            window(r, mask_last_chunk(sw), vw_ref[0:w, :])

    @pl.when(jd > 0)
    def _():
        for r in range(NSA_SUB):
            w0 = pl.multiple_of(step0 + r * tq - WINDOW, tq)
            sw = _dot_nt(qaug_ref[r][:, 0:NSA_HEAD_DIM], kw_ref[pl.ds(w0, WIN_SLAB), :])
            sw = jnp.concatenate([sw[:, :LANES] + tri_far, sw[:, LANES:WIN_SLAB - LANES],
                                  sw[:, WIN_SLAB - LANES:] + tri_diag], axis=1)
            window(r, sw, vw_ref[pl.ds(w0, WIN_SLAB), :])

    n_gate = 3 * NSA_HEADS
    spread = jnp.where(lax.broadcasted_iota(jnp.int32, (LANES, n_gate * LANES), 0)
                       == (lax.broadcasted_iota(jnp.int32, (LANES, n_gate * LANES), 1) >> int(math.log2(LANES))),
                       1.0, 0.0).astype(BF16)
    g_hi, g_lo = _split(_sigmoid(misc_ref[...]))
    gates = _dot(g_hi, spread) + _dot(g_lo, spread)
    for r in range(NSA_SUB):
        o_s = acc_ref[r] * (1.0 / l_ref[r])
        o_c = oc_ref[r]
        o_w = ow_ref[r]
        g_r = gates[r * tq:(r + 1) * tq]
        outs = []
        for h in range(NSA_HEADS):
            hr = slice(h * tq, (h + 1) * tq)
            gc, gs, gw = (g_r[:, (3 * h + k) * LANES:(3 * h + k + 1) * LANES] for k in range(3))
            o_h = gc * o_c[hr] + gs * o_s[hr] + gw * o_w[hr]
            outs.append(_rms(o_h, og_ref[:, h * NSA_HEAD_DIM:(h + 1) * NSA_HEAD_DIM]))
        o_ref[r * tq:(r + 1) * tq, :] = jnp.concatenate(outs, axis=1).astype(BF16)


def _nsa(nq, kc, vct, ks, vs, kw, vw, misc, ovt, og):
    b, s, wq = nq.shape
    ts = NSA_SUB * NSA_TQ
    rows = NSA_HEADS * NSA_TQ

    def per_batch(a):
        return pl.BlockSpec((None,) + a.shape[1:], lambda i, j: (i, 0, 0))

    def per_tile(a):
        return pl.BlockSpec((None, ts, a.shape[2]), lambda i, j: (i, j, 0))

    def full(a):
        return pl.BlockSpec(a.shape, lambda i, j: (0, 0))

    stat = pltpu.VMEM((NSA_SUB, rows, LANES), F32)
    return pl.pallas_call(
        _nsa_kernel,
        grid=(b, s // ts),
        in_specs=[per_tile(nq), per_batch(kc), per_batch(vct), per_batch(ks), per_batch(vs), per_batch(kw),
                  per_batch(vw), per_tile(misc), full(ovt), full(og)],
        out_specs=pl.BlockSpec((None, ts, wq), lambda i, j: (i, j, 0)),
        out_shape=jax.ShapeDtypeStruct((b, s, wq), BF16),
        scratch_shapes=[pltpu.VMEM((NSA_SUB, rows, 2 * NSA_HEAD_DIM), BF16), stat, stat,
                        pltpu.VMEM((NSA_SUB, rows, NSA_HEAD_DIM), F32),
                        pltpu.VMEM((NSA_SUB, rows, NSA_HEAD_DIM), F32),
                        pltpu.VMEM((NSA_SUB, rows, NSA_HEAD_DIM), F32)],
        compiler_params=_cparams(("parallel", "arbitrary")),
        name="nsa",
    )(nq, kc, vct, ks, vs, kw, vw, misc, ovt, og)


def _gla_kernel(q_ref, k_ref, v_ref, og_ref, misc_ref, up_ref, bias_ref, g_ref, o_ref, st_ref):
    c = GLA_CHUNK
    kw = GLA_HEADS * GLA_KEY_DIM
    tg = q_ref.shape[1]
    n_chunk = tg // c
    log_c = int(math.log2(c))

    @pl.when(pl.program_id(1) == 0)
    def _():
        st_ref[...] = jnp.zeros_like(st_ref)

    ri = lax.broadcasted_iota(jnp.int32, (tg, tg), 0)
    ci = lax.broadcasted_iota(jnp.int32, (tg, tg), 1)
    low = ((ri >> log_c) == (ci >> log_c)) & (ci <= ri)
    causal = jnp.where(low, 1.0, 0.0)
    tri = causal.astype(BF16)
    lane_head = lax.broadcasted_iota(jnp.int32, (1, kw), 1) >> int(math.log2(GLA_KEY_DIM))
    head_mask = [lane_head == h for h in range(GLA_HEADS)]
    u_hi, u_lo = _split(up_ref[...])

    def stack_heads(x):
        return jnp.concatenate([jnp.where(head_mask[h], x, 0.0) for h in range(GLA_HEADS)], axis=0)

    for bb in range(q_ref.shape[0]):
        m_hi, m_lo = _split(misc_ref[bb])
        g_pre = _dot(m_hi, u_hi) + _dot(m_lo, u_hi) + _dot(m_hi, u_lo) + bias_ref[...]
        g_log = (jnp.minimum(g_pre, 0.0) - jnp.log(1.0 + jnp.exp(-jnp.abs(g_pre)))) * (1.0 / GLA_GATE_NORM)

        g_hi, g_lo = _split(g_log)
        bcum = _dot(tri, g_hi) + _dot(tri, g_lo)
        b_last = jnp.concatenate([jnp.broadcast_to(bcum[(cc + 1) * c - 1:(cc + 1) * c, :], (c, kw))
                                  for cc in range(n_chunk)], axis=0)
        qf = q_ref[bb].astype(F32)
        kf = k_ref[bb].astype(F32)
        v = v_ref[bb]
        q_dec = qf * jnp.exp(bcum) * (GLA_KEY_DIM ** -0.5)
        k_dec = (kf * jnp.exp(-bcum)).astype(BF16)
        k_end = kf * jnp.exp(b_last - bcum)
        intra = []
        for h in range(GLA_HEADS):
            q_h = jnp.where(head_mask[h], q_dec, 0.0).astype(BF16)
            a = (_dot_nt(q_h, k_dec) * causal).astype(BF16)
            intra.append(_dot(a, v[:, h * GLA_VAL_DIM:(h + 1) * GLA_VAL_DIM]))

        st = st_ref[bb]
        inter = []
        for cc in range(n_chunk):
            r = slice(cc * c, (cc + 1) * c)
            inter.append(_dot_nt(stack_heads(q_dec[r]).astype(BF16), st.astype(BF16)))
            km = stack_heads(k_end[r]).astype(BF16)
            vst = jnp.concatenate([v[r, h * GLA_VAL_DIM:(h + 1) * GLA_VAL_DIM] for h in range(GLA_HEADS)],
                                  axis=0)
            st = st * jnp.exp(b_last[cc * c:cc * c + 1, :]) + _dot_tn(vst, km)
        st_ref[bb] = st

        for h in range(GLA_HEADS):
            vs = slice(h * GLA_VAL_DIM, (h + 1) * GLA_VAL_DIM)
            inter_h = jnp.concatenate([inter[cc][h * c:(h + 1) * c] for cc in range(n_chunk)], axis=0)
            gate = og_ref[bb, :, vs].astype(F32)
            o_ref[bb, :, vs] = (_rms(intra[h] + inter_h, g_ref[:, vs]) * (gate * _sigmoid(gate))).astype(BF16)


def _gla(gq, gk, gv, gog, misc, up_pad, bias, g):
    b, s, _ = gq.shape
    tg = min(GLA_TG, s)
    nb = GLA_NB if b % GLA_NB == 0 else 1

    def per_tile(a):
        return pl.BlockSpec((nb, tg, a.shape[2]), lambda i, j: (i, j, 0))

    def full(a):
        return pl.BlockSpec(a.shape, lambda i, j: (0, 0))

    return pl.pallas_call(
        _gla_kernel,
        grid=(b // nb, s // tg),
        in_specs=[per_tile(gq), per_tile(gk), per_tile(gv), per_tile(gog), per_tile(misc),
                  full(up_pad), full(bias), full(g)],
        out_specs=per_tile(gv),
        out_shape=jax.ShapeDtypeStruct(gv.shape, BF16),
        scratch_shapes=[pltpu.VMEM((nb, GLA_VAL_DIM, GLA_HEADS * GLA_KEY_DIM), F32)],
        compiler_params=_cparams(("parallel", "arbitrary")),
        name="gla",
    )(gq, gk, gv, gog, misc, up_pad, bias, g)


def _mixout_kernel(on_ref, ogl_ref, h_ref, wa_ref, wb_ref, g_ref, o_ref):
    mix = _dot(on_ref[...], wa_ref[...]) + _dot(ogl_ref[...], wb_ref[...])
    o_ref[...] = h_ref[...] + _rms(mix, g_ref[...])


def _mixout(o_nsa, o_gla, h, wa, wb, g):
    t, d = h.shape
    tm = min(TOK_TM, t)
    w = o_nsa.shape[1]
    return pl.pallas_call(
        _mixout_kernel,
        grid=(t // tm,),
        in_specs=[
            pl.BlockSpec((tm, w), lambda i: (i, 0)),
            pl.BlockSpec((tm, w), lambda i: (i, 0)),
            pl.BlockSpec((tm, d), lambda i: (i, 0)),
            pl.BlockSpec(wa.shape, lambda i: (0, 0)),
            pl.BlockSpec(wb.shape, lambda i: (0, 0)),
            pl.BlockSpec((1, d), lambda i: (0, 0)),
        ],
        out_specs=pl.BlockSpec((tm, d), lambda i: (i, 0)),
        out_shape=jax.ShapeDtypeStruct((t, d), F32),
        compiler_params=_cparams(("parallel",)),
        name="mixout",
    )(o_nsa, o_gla, h, wa, wb, g)


def _ple_kernel(h_ref, p_ref, wp_ref, wg_ref, g_ref, o_ref):
    h = h_ref[...]
    e = _dot(p_ref[...].astype(BF16), wp_ref[...])
    gate = _sigmoid(_dot(h.astype(BF16), wg_ref[...]))
    o_ref[...] = h + _rms(e * gate, g_ref[...])


def _ple(h, p, wp, wg, g):
    t, d = h.shape
    tm = min(TOK_TM, t)
    return pl.pallas_call(
        _ple_kernel,
        grid=(t // tm,),
        in_specs=[
            pl.BlockSpec((tm, d), lambda i: (i, 0)),
            pl.BlockSpec((tm, p.shape[1]), lambda i: (i, 0)),
            pl.BlockSpec(wp.shape, lambda i: (0, 0)),
            pl.BlockSpec(wg.shape, lambda i: (0, 0)),
            pl.BlockSpec((1, d), lambda i: (0, 0)),
        ],
        out_specs=pl.BlockSpec((tm, d), lambda i: (i, 0)),
        out_shape=jax.ShapeDtypeStruct((t, d), F32),
        compiler_params=_cparams(("parallel",)),
        name="ple",
    )(h, p, wp, wg, g)


def _overlap_matrix(n_rows):
    n = np.arange(n_rows)[:, None] * CMP_STRIDE
    s = np.arange(LANES)[None, :] * SLC_BLOCK
    return ((n < s + SLC_BLOCK) & (n + CMP_BLOCK > s)).astype(np.float32)


def _layer(h, p, seq, batch, ffn1_pre_g, ffn1_post_g, ffn1_w_gate, ffn1_w_up, ffn1_w_down, mix_pre_g, mix_post_g,
           w_in, cmp_k_pe, cmp_k_w1, cmp_k_w2, cmp_v_pe, cmp_v_w1, cmp_v_w2, nsa_out_g, gla_gate_up, gla_gate_bias,
           gla_out_g, w_out, ffn2_pre_g, ffn2_post_g, ffn2_w_gate, ffn2_w_up, ffn2_w_down, ple_proj, ple_gate,
           ple_post_g):
    d = h.shape[1]
    row = lambda a: a.reshape(1, -1).astype(F32)
    bf = lambda a: a.astype(BF16)

    h1 = _ffn(h, row(ffn1_pre_g), row(ffn1_post_g), bf(ffn1_w_gate), bf(ffn1_w_up), bf(ffn1_w_down))

    o_gate = 512 + 6 * 128
    o_gq = o_gate + 3 * NSA_HEADS
    o_gk = o_gq + GLA_HEADS * GLA_KEY_DIM
    o_gv = o_gk + GLA_HEADS * GLA_KEY_DIM
    o_gd = o_gv + GLA_HEADS * GLA_VAL_DIM
    o_og = o_gd + GLA_GATE_RANK
    pad = jnp.zeros((d, LANES - 3 * NSA_HEADS - GLA_GATE_RANK), w_in.dtype)
    w_all = bf(jnp.concatenate([w_in[:, :o_gate], w_in[:, o_gate:o_gq], w_in[:, o_gd:o_og], pad,
                                w_in[:, o_gq:o_gd], w_in[:, o_og:]], axis=1))
    nq, kc_in, vc_in, ks, vs, kw, vw, misc, gq, gk, gv, gog = _proj(h1, row(mix_pre_g), w_all, seq)

    n_grp = seq // CMP_STRIDE
    grp_w = CMP_STRIDE * NSA_HEAD_DIM
    kc, vc = _compress(
        kc_in.reshape(batch, n_grp, grp_w), vc_in.reshape(batch, n_grp, grp_w),
        cmp_k_pe.reshape(2, grp_w).astype(F32), bf(cmp_k_w1.reshape(2 * grp_w, -1)), bf(cmp_k_w2),
        cmp_v_pe.reshape(2, grp_w).astype(F32), bf(cmp_v_w1.reshape(2 * grp_w, -1)), bf(cmp_v_w2))

    b3 = lambda a: a.reshape(batch, seq, a.shape[-1])
    ovt = jnp.asarray(_overlap_matrix(n_grp).T, BF16)
    o_nsa = _nsa(b3(nq), kc, vc, b3(ks), b3(vs), b3(kw), b3(vw), b3(misc), ovt, row(nsa_out_g))

    up_pad = jnp.zeros((LANES, GLA_HEADS * GLA_KEY_DIM), F32)
    up_pad = up_pad.at[_MISC_GDOWN:_MISC_GDOWN + GLA_GATE_RANK].set(gla_gate_up.astype(F32))
    o_gla = _gla(b3(gq), b3(gk), b3(gv), b3(gog), b3(misc), up_pad, row(gla_gate_bias), row(gla_out_g))

    w_o = bf(w_out)
    half = NSA_HEADS * NSA_HEAD_DIM
    h2 = _mixout(o_nsa.reshape(-1, half), o_gla.reshape(-1, half), h1, w_o[:half], w_o[half:], row(mix_post_g))
    h3 = _ffn(h2, row(ffn2_pre_g), row(ffn2_post_g), bf(ffn2_w_gate), bf(ffn2_w_up), bf(ffn2_w_down))
    return _ple(h3, p, bf(ple_proj), bf(ple_gate), row(ple_post_g))


def kernel(x, p, ffn1_pre_g, ffn1_post_g, ffn1_w_gate, ffn1_w_up, ffn1_w_down, mix_pre_g, mix_post_g, w_in, cmp_k_pe, cmp_k_w1, cmp_k_w2, cmp_v_pe, cmp_v_w1, cmp_v_w2, nsa_out_g, gla_gate_up, gla_gate_bias, gla_out_g, w_out, ffn2_pre_g, ffn2_post_g, ffn2_w_gate, ffn2_w_up, ffn2_w_down, ple_proj, ple_gate, ple_post_g):
    batch, seq, d = x.shape
    depth = p.shape[0]
    assert seq % NSA_TK == 0 and seq >= WIN_SLAB and seq // SLC_BLOCK <= LANES
    h = x.reshape(batch * seq, d)
    params = (ffn1_pre_g, ffn1_post_g, ffn1_w_gate, ffn1_w_up, ffn1_w_down, mix_pre_g, mix_post_g, w_in, cmp_k_pe,
              cmp_k_w1, cmp_k_w2, cmp_v_pe, cmp_v_w1, cmp_v_w2, nsa_out_g, gla_gate_up, gla_gate_bias, gla_out_g,
              w_out, ffn2_pre_g, ffn2_post_g, ffn2_w_gate, ffn2_w_up, ffn2_w_down, ple_proj, ple_gate, ple_post_g)
    for i in range(depth):
        h = _layer(h, p[i].reshape(batch * seq, -1), seq, batch, *[a[i] for a in params])
    return h.reshape(batch, seq, d)
```

```python
import functools
import math

import numpy as np
import jax
import jax.numpy as jnp
from jax import lax
from jax.experimental import pallas as pl
from jax.experimental.pallas import tpu as pltpu

F32 = jnp.float32
BF16 = jnp.bfloat16

EPS = 1e-6
NSA_HEADS = 4
NSA_HEAD_DIM = 128
CMP_BLOCK = 32
CMP_STRIDE = 16
SLC_BLOCK = 64
SLC_TOPK = 16
WINDOW = 512
FORCE_BONUS = 1e4
NEG = -1e30
GLA_HEADS = 4
GLA_KEY_DIM = 64
GLA_VAL_DIM = 128
GLA_GATE_RANK = 16
GLA_GATE_NORM = 16.0
GLA_CHUNK = 64

LOG2E = 1.4426950408889634
N_FORCED = 3
N_BRANCH = 3

LANES = 128
VMEM_LIMIT = 56 * 1024 * 1024

FFN_TM = 1024
FFN_TF = 256
TOK_TM = 512
NSA_TQ = 128
NSA_SUB = 4
NSA_TK = 512
NSA_UNROLL = 2
GLA_TG = 256
GLA_NB = 2
WIN_SLAB = WINDOW + NSA_TQ


def _dot(a, b):
    return jnp.dot(a, b, preferred_element_type=F32)


def _dot_nt(a, b):
    return lax.dot_general(a, b, (((1,), (1,)), ((), ())), preferred_element_type=F32)


def _dot_tn(a, b):
    return lax.dot_general(a, b, (((0,), (0,)), ((), ())), preferred_element_type=F32)


def _split(x):
    hi = x.astype(BF16)
    lo = (x - hi.astype(F32)).astype(BF16)
    return hi, lo


def _rms(x, g):
    return x * lax.rsqrt(jnp.mean(x * x, axis=-1, keepdims=True) + EPS) * g


def _sigmoid(x):
    return 1.0 / (1.0 + jnp.exp(-x))


def _cparams(sem):
    return pltpu.CompilerParams(dimension_semantics=sem, vmem_limit_bytes=VMEM_LIMIT)


def _ffn_kernel(n_mix, n_ple, *refs):
    h_ref, pre_ref, post_ref, wg_ref, wu_ref, wd_ref = refs[:6]
    mix_refs = refs[6:6 + n_mix]
    ple_refs = refs[6 + n_mix:6 + n_mix + n_ple]
    o_ref = refs[-1]
    h = h_ref[...]
    if n_mix:
        on_ref, ogl_ref, wa_ref, wb_ref, mg_ref = mix_refs
        h = h + _rms(_dot(on_ref[...], wa_ref[...]) + _dot(ogl_ref[...], wb_ref[...]), mg_ref[...])
    u = _rms(h, pre_ref[...]).astype(BF16)
    acc = None
    for c in range(wg_ref.shape[1] // FFN_TF):
        sl = slice(c * FFN_TF, (c + 1) * FFN_TF)
        a = _dot(u, wg_ref[:, sl])
        b = _dot(u, wu_ref[:, sl])
        part = _dot((a * _sigmoid(a) * b).astype(BF16), wd_ref[sl, :])
        acc = part if acc is None else acc + part
    h = h + 0.5 * _rms(acc, post_ref[...])
    if n_ple:
        p_ref, wp_ref, wgate_ref, pg_ref = ple_refs
        e = _dot(p_ref[...].astype(BF16), wp_ref[...])
        h = h + _rms(e * _sigmoid(_dot(h.astype(BF16), wgate_ref[...])), pg_ref[...])
    o_ref[...] = h


def _resident(a):
    return pl.BlockSpec(a.shape, lambda *_: (0,) * a.ndim, pipeline_mode=pl.Buffered(1))


def _ffn(h, pre_g, post_g, wg, wu, wd, mix=(), ple=()):
    t, d = h.shape
    tm = min(FFN_TM, t)

    def tile(a):
        return pl.BlockSpec((tm, a.shape[1]), lambda i: (i, 0))

    args = [h, pre_g, post_g, wg, wu, wd, *mix, *ple]
    specs = [tile(h)] + [_resident(a) for a in (pre_g, post_g, wg, wu, wd)]
    if mix:
        specs += [tile(mix[0]), tile(mix[1])] + [_resident(a) for a in mix[2:]]
    if ple:
        specs += [tile(ple[0])] + [_resident(a) for a in ple[1:]]
    return pl.pallas_call(
        functools.partial(_ffn_kernel, len(mix), len(ple)),
        grid=(t // tm,),
        in_specs=specs,
        out_specs=tile(h),
        out_shape=jax.ShapeDtypeStruct((t, d), F32),
        compiler_params=_cparams(("parallel",)),
        name="ffn",
    )(*args)


_P_NQ = (0, 512)
_P_KC = (512, 640)
_P_VC = (640, 768)
_P_KS = (768, 896)
_P_VS = (896, 1024)
_P_KW = (1024, 1152)
_P_VW = (1152, 1280)
_P_MISC = (1280, 1408)
_P_GQ = (1408, 1664)
_P_GK = (1664, 1920)
_P_GV = (1920, 2432)
_P_GOG = (2432, 2944)
_P_WIDTH = 2944
_MISC_GDOWN = 3 * NSA_HEADS


def _proj_kernel(seq, h_ref, g_ref, w_ref, nq_ref, kc_ref, vc_ref, ks_ref, vs_ref, kw_ref, vw_ref,
                 misc_ref, gq_ref, gk_ref, gv_ref, gog_ref):
    tm = h_ref.shape[0]
    u = _rms(h_ref[...], g_ref[...]).astype(BF16)

    def grp(lohi):
        return _dot(u, w_ref[:, lohi[0]:lohi[1]])

    nq_ref[...] = (grp(_P_NQ) * (NSA_HEAD_DIM ** -0.5 * LOG2E)).astype(BF16)
    kv_c = grp((_P_KC[0], _P_VC[1])).astype(BF16)
    kc_ref[...] = kv_c[:, 0:LANES]
    vc_ref[...] = kv_c[:, LANES:2 * LANES]
    base = (pl.program_id(0) % (seq // tm)) * tm
    row = lax.broadcasted_iota(jnp.int32, (tm, LANES), 0)
    lane = lax.broadcasted_iota(jnp.int32, (tm, LANES), 1)
    onehot = jnp.where(lane == ((base + row) >> int(math.log2(SLC_BLOCK))), 1.0, 0.0).astype(BF16)
    kv_s = grp((_P_KS[0], _P_VS[1])).astype(BF16)
    ks_ref[:, 0:LANES] = kv_s[:, 0:LANES]
    ks_ref[:, LANES:2 * LANES] = onehot
    vs_ref[...] = kv_s[:, LANES:2 * LANES]
    kv_w = grp((_P_KW[0], _P_VW[1])).astype(BF16)
    kw_ref[...] = kv_w[:, 0:LANES]
    vw_ref[...] = kv_w[:, LANES:2 * LANES]
    misc_ref[...] = grp(_P_MISC)
    gq_ref[...] = grp(_P_GQ).astype(BF16)
    gk_ref[...] = grp(_P_GK).astype(BF16)
    gv_ref[...] = grp(_P_GV).astype(BF16)
    gog_ref[...] = grp(_P_GOG).astype(BF16)


def _proj(h, g, w_all, seq):
    t, d = h.shape
    tm = min(TOK_TM, seq)
    widths = [(512, BF16), (128, BF16), (128, BF16), (256, BF16), (128, BF16), (128, BF16), (128, BF16),
              (128, F32), (256, BF16), (256, BF16), (512, BF16), (512, BF16)]
    return pl.pallas_call(
        functools.partial(_proj_kernel, seq),
        grid=(t // tm,),
        in_specs=[
            pl.BlockSpec((tm, d), lambda i: (i, 0)),
            pl.BlockSpec((1, d), lambda i: (0, 0)),
            pl.BlockSpec((d, _P_WIDTH), lambda i: (0, 0)),
        ],
        out_specs=[pl.BlockSpec((tm, w), lambda i: (i, 0)) for w, _ in widths],
        out_shape=[jax.ShapeDtypeStruct((t, w), dt) for w, dt in widths],
        compiler_params=_cparams(("parallel",)),
        name="proj",
    )(h, g, w_all)


def _compress_kernel(k_ref, v_ref, pek_ref, w1k_ref, w2k_ref, pev_ref, w1v_ref, w2v_ref, kc_ref, vc_ref):
    half = CMP_STRIDE * NSA_HEAD_DIM

    def one(x_ref, pe_ref, w1_ref, w2_ref):
        x = x_ref[...].astype(F32)
        n = x.shape[0]
        a = _dot((x + pe_ref[0:1, :]).astype(BF16), w1_ref[0:half, :])
        b = _dot((x + pe_ref[1:2, :]).astype(BF16), w1_ref[half:2 * half, :])
        hid = a + pltpu.roll(b, n - 1, 0)
        act = (hid * _sigmoid(hid)).astype(BF16)
        return _dot(act, w2_ref[...])

    kc_ref[...] = one(k_ref, pek_ref, w1k_ref, w2k_ref).astype(BF16)
    vc_ref[...] = one(v_ref, pev_ref, w1v_ref, w2v_ref).T.astype(BF16)


def _compress(k2, v2, pek, w1k, w2k, pev, w1v, w2v):
    b, n, w = k2.shape
    dk = w2k.shape[1]
    big = pl.BlockSpec((None, n, w), lambda i: (i, 0, 0))

    def full(a):
        return pl.BlockSpec(a.shape, lambda i: (0,) * a.ndim)

    return pl.pallas_call(
        _compress_kernel,
        grid=(b,),
        in_specs=[big, big, full(pek), full(w1k), full(w2k), full(pev), full(w1v), full(w2v)],
        out_specs=[pl.BlockSpec((None, n, dk), lambda i: (i, 0, 0)), pl.BlockSpec((None, dk, n), lambda i: (i, 0, 0))],
        out_shape=[jax.ShapeDtypeStruct((b, n, dk), BF16), jax.ShapeDtypeStruct((b, dk, n), BF16)],
        compiler_params=_cparams(("parallel",)),
        name="compress",
    )(k2, v2, pek, w1k, w2k, pev, w1v, w2v)


def _nsa_kernel(q_ref, kc_ref, vct_ref, ks_ref, vs_ref, kw_ref, vw_ref, misc_ref, ovt_ref, og_ref, o_ref,
                qaug_ref, m_ref, acc_ref, oc_ref, gate_ref):
    tq = NSA_TQ
    rows = NSA_HEADS * tq
    jd = pl.program_id(1)
    step0 = jd * NSA_TK
    n_c = kc_ref.shape[0]
    log_g = int(math.log2(CMP_STRIDE))
    row_i = lax.broadcasted_iota(jnp.int32, (rows, LANES), 0) & (tq - 1)
    lane = lax.broadcasted_iota(jnp.int32, (rows, LANES), 1)
    lane_q = lax.broadcasted_iota(jnp.int32, (1, rows), 1) & (tq - 1)
    tri_diag = jnp.where(lane <= row_i, 0.0, NEG)
    tri_far = jnp.where(lane > row_i, 0.0, NEG)
    q_grp = jnp.where(lane == ((row_i + 1) >> log_g), 1.0, 0.0).astype(BF16)
    n_minus_g = (lax.broadcasted_iota(jnp.int32, (n_c, LANES), 0)
                 - lax.broadcasted_iota(jnp.int32, (n_c, LANES), 1))

    n_gate = N_BRANCH * NSA_HEADS
    spread = jnp.where(lax.broadcasted_iota(jnp.int32, (LANES, n_gate * LANES), 0)
                       == (lax.broadcasted_iota(jnp.int32, (LANES, n_gate * LANES), 1) >> int(math.log2(LANES))),
                       1.0, 0.0).astype(BF16)
    g_hi, g_lo = _split(_sigmoid(misc_ref[...]))
    gate_ref[...] = _dot(g_hi, spread) + _dot(g_lo, spread)

    def heads_to_rows(q):
        return jnp.concatenate([q[:, h * NSA_HEAD_DIM:(h + 1) * NSA_HEAD_DIM] for h in range(NSA_HEADS)], axis=0)

    for r in range(NSA_SUB):
        q0 = step0 + r * tq
        qh = heads_to_rows(q_ref[r * tq:(r + 1) * tq, :])
        vis = jnp.where(n_minus_g <= (q0 >> log_g) - 2, 0.0, NEG).astype(BF16)
        st = _dot_nt(jnp.concatenate([kc_ref[...], vis], axis=1), jnp.concatenate([qh, q_grp], axis=1))
        et = jnp.exp2(st - jnp.max(st, axis=0, keepdims=True))
        inv = jnp.where(q0 + lane_q >= CMP_BLOCK - 1, 1.0 / jnp.sum(et, axis=0, keepdims=True), 0.0)
        p_t = et * inv
        oc_ref[r] = _dot(vct_ref[...], p_t.astype(BF16)).T
        p_sum = p_t[:, 0:tq]
        for h in range(1, NSA_HEADS):
            p_sum = p_sum + p_t[:, h * tq:(h + 1) * tq]
        p_hi, p_lo = _split(p_sum)
        imp = _dot(ovt_ref[...], p_hi) + _dot(ovt_ref[...], p_lo)
        blk = lax.broadcasted_iota(jnp.int32, (LANES, tq), 0)
        cur = (q0 + lax.broadcasted_iota(jnp.int32, (LANES, tq), 1)) >> int(math.log2(SLC_BLOCK))
        forced = (blk == 0) | (blk == cur) | (blk == cur - 1)
        val = jnp.where(forced | (blk > cur), -1.0, imp)
        blk_f = blk.astype(F32)
        bias_t = jnp.where(forced, 0.0, NEG)
        for _ in range(SLC_TOPK - N_FORCED):
            top = jnp.max(val, axis=0, keepdims=True)
            first = jnp.min(jnp.where(val == top, blk_f, float(LANES)), axis=0, keepdims=True)
            pick = blk_f == first
            bias_t = jnp.where(pick, 0.0, bias_t)
            val = jnp.where(pick, -2.0, val)
        bias = bias_t.T.astype(BF16)
        qaug_ref[r] = jnp.concatenate([qh, jnp.concatenate([bias] * NSA_HEADS, axis=0)], axis=1)
        m_ref[r] = jnp.full((rows, LANES), NEG, F32)
        acc_ref[r] = jnp.zeros((rows, 2 * NSA_HEAD_DIM), F32)

    def mask_last_chunk(sc):
        w = sc.shape[1]
        if w == LANES:
            return sc + tri_diag
        return jnp.concatenate([sc[:, :w - LANES], sc[:, w - LANES:] + tri_diag], axis=1)

    def with_ones(v):
        return jnp.concatenate([v, jnp.ones(v.shape, BF16)], axis=1)

    def flash(r, kt, vt, diag):
        sc = _dot_nt(qaug_ref[r], kt)
        if diag:
            sc = mask_last_chunk(sc)
        m_i = m_ref[r]
        m_n = jnp.maximum(m_i, jnp.max(sc, axis=1, keepdims=True))
        alpha = jnp.exp2(m_i - m_n)
        p = jnp.exp2(sc - jnp.concatenate([m_n] * (sc.shape[1] // LANES), axis=1))
        acc_ref[r] = jnp.concatenate([alpha, alpha], axis=1) * acc_ref[r] + _dot(p.astype(BF16), with_ones(vt))
        m_ref[r] = m_n

    def full_tiles(j0, n):
        for t in range(n):
            k0 = pl.multiple_of((j0 + t) * NSA_TK, NSA_TK)
            kt = ks_ref[pl.ds(k0, NSA_TK), :]
            vt = vs_ref[pl.ds(k0, NSA_TK), :]
            for r in range(NSA_SUB):
                flash(r, kt, vt, False)

    def tile_group(j, carry):
        full_tiles(j * NSA_UNROLL, NSA_UNROLL)
        return carry

    lax.fori_loop(0, jd // NSA_UNROLL, tile_group, 0)
    for rem in range(1, NSA_UNROLL):
        @pl.when(jd % NSA_UNROLL == rem)
        def _(rem=rem):
            full_tiles(jd - rem, rem)

    def window(r, sw, vw):
        ew = jnp.exp2(sw - jnp.max(sw, axis=1, keepdims=True))
        pv = _dot(ew.astype(BF16), with_ones(vw))
        return pv[:, 0:NSA_HEAD_DIM] * (1.0 / pv[:, NSA_HEAD_DIM:])

    def finish(first_step):
        d0 = pl.multiple_of(step0, NSA_TK)
        for r in range(NSA_SUB):
            w = (r + 1) * tq
            flash(r, ks_ref[pl.ds(d0, w), :], vs_ref[pl.ds(d0, w), :], True)
        for r in range(NSA_SUB):
            qh = qaug_ref[r][:, 0:NSA_HEAD_DIM]
            if first_step:
                w = (r + 1) * tq
                o_w = window(r, mask_last_chunk(_dot_nt(qh, kw_ref[0:w, :])), vw_ref[0:w, :])
            else:
                w0 = pl.multiple_of(step0 + r * tq - WINDOW, tq)
                sw = _dot_nt(qh, kw_ref[pl.ds(w0, WIN_SLAB), :])
                sw = jnp.concatenate([sw[:, :LANES] + tri_far, sw[:, LANES:WIN_SLAB - LANES],
                                      sw[:, WIN_SLAB - LANES:] + tri_diag], axis=1)
                o_w = window(r, sw, vw_ref[pl.ds(w0, WIN_SLAB), :])
            acc = acc_ref[r]
            o_s = acc[:, 0:NSA_HEAD_DIM] * (1.0 / acc[:, NSA_HEAD_DIM:])
            o_c = oc_ref[r]
            outs = []
            for h in range(NSA_HEADS):
                hr = slice(h * tq, (h + 1) * tq)
                gc, gs, gw = (gate_ref[r * tq:(r + 1) * tq,
                                       (N_BRANCH * h + k) * LANES:(N_BRANCH * h + k + 1) * LANES]
                              for k in range(N_BRANCH))
                o_h = gc * o_c[hr] + gs * o_s[hr] + gw * o_w[hr]
                outs.append(_rms(o_h, og_ref[:, h * NSA_HEAD_DIM:(h + 1) * NSA_HEAD_DIM]))
            o_ref[r * tq:(r + 1) * tq, :] = jnp.concatenate(outs, axis=1).astype(BF16)

    pl.when(jd == 0)(functools.partial(finish, True))
    pl.when(jd > 0)(functools.partial(finish, False))


def _nsa(nq, kc, vct, ks, vs, kw, vw, misc, ovt, og):
    b, s, wq = nq.shape
    ts = NSA_SUB * NSA_TQ
    rows = NSA_HEADS * NSA_TQ

    def per_batch(a):
        return pl.BlockSpec((None,) + a.shape[1:], lambda i, j: (i, 0, 0))

    def per_tile(a):
        return pl.BlockSpec((None, ts, a.shape[2]), lambda i, j: (i, j, 0))

    def full(a):
        return pl.BlockSpec(a.shape, lambda i, j: (0, 0))

    return pl.pallas_call(
        _nsa_kernel,
        grid=(b, s // ts),
        in_specs=[per_tile(nq), per_batch(kc), per_batch(vct), per_batch(ks), per_batch(vs), per_batch(kw),
                  per_batch(vw), per_tile(misc), full(ovt), full(og)],
        out_specs=pl.BlockSpec((None, ts, wq), lambda i, j: (i, j, 0)),
        out_shape=jax.ShapeDtypeStruct((b, s, wq), BF16),
        scratch_shapes=[pltpu.VMEM((NSA_SUB, rows, 2 * NSA_HEAD_DIM), BF16),
                        pltpu.VMEM((NSA_SUB, rows, LANES), F32),
                        pltpu.VMEM((NSA_SUB, rows, 2 * NSA_HEAD_DIM), F32),
                        pltpu.VMEM((NSA_SUB, rows, NSA_HEAD_DIM), F32),
                        pltpu.VMEM((ts, N_BRANCH * NSA_HEADS * LANES), F32)],
        compiler_params=_cparams(("parallel", "arbitrary")),
        name="nsa",
    )(nq, kc, vct, ks, vs, kw, vw, misc, ovt, og)


def _gla_kernel(q_ref, k_ref, v_ref, og_ref, misc_ref, up_ref, bias_ref, g_ref, o_ref, st_ref):
    c = GLA_CHUNK
    kw = GLA_HEADS * GLA_KEY_DIM
    tg = q_ref.shape[1]
    n_chunk = tg // c
    log_c = int(math.log2(c))

    @pl.when(pl.program_id(1) == 0)
    def _():
        st_ref[...] = jnp.zeros_like(st_ref)

    ri = lax.broadcasted_iota(jnp.int32, (tg, tg), 0)
    ci = lax.broadcasted_iota(jnp.int32, (tg, tg), 1)
    low = ((ri >> log_c) == (ci >> log_c)) & (ci <= ri)
    causal = jnp.where(low, 1.0, 0.0)
    tri = causal.astype(BF16)
    lane_head = lax.broadcasted_iota(jnp.int32, (1, kw), 1) >> int(math.log2(GLA_KEY_DIM))
    head_mask = [lane_head == h for h in range(GLA_HEADS)]
    u_hi, u_lo = _split(up_ref[...])

    def stack_heads(x):
        return jnp.concatenate([jnp.where(head_mask[h], x, 0.0) for h in range(GLA_HEADS)], axis=0)

    for bb in range(q_ref.shape[0]):
        m_hi, m_lo = _split(misc_ref[bb])
        g_pre = _dot(m_hi, u_hi) + _dot(m_lo, u_hi) + _dot(m_hi, u_lo) + bias_ref[...]
        g_log = (jnp.minimum(g_pre, 0.0) - jnp.log(1.0 + jnp.exp(-jnp.abs(g_pre)))) * (1.0 / GLA_GATE_NORM)

        g_hi, g_lo = _split(g_log)
        bcum = _dot(tri, g_hi) + _dot(tri, g_lo)
        b_last = jnp.concatenate([jnp.broadcast_to(bcum[(cc + 1) * c - 1:(cc + 1) * c, :], (c, kw))
                                  for cc in range(n_chunk)], axis=0)
        qf = q_ref[bb].astype(F32)
        kf = k_ref[bb].astype(F32)
        v = v_ref[bb]
        q_dec = qf * jnp.exp(bcum) * (GLA_KEY_DIM ** -0.5)
        k_dec = (kf * jnp.exp(-bcum)).astype(BF16)
        k_end = kf * jnp.exp(b_last - bcum)
        intra = []
        for h in range(GLA_HEADS):
            q_h = jnp.where(head_mask[h], q_dec, 0.0).astype(BF16)
            a = (_dot_nt(q_h, k_dec) * causal).astype(BF16)
            intra.append(_dot(a, v[:, h * GLA_VAL_DIM:(h + 1) * GLA_VAL_DIM]))

        st = st_ref[bb]
        inter = []
        for cc in range(n_chunk):
            r = slice(cc * c, (cc + 1) * c)
            inter.append(_dot_nt(stack_heads(q_dec[r]).astype(BF16), st.astype(BF16)))
            km = stack_heads(k_end[r]).astype(BF16)
            vst = jnp.concatenate([v[r, h * GLA_VAL_DIM:(h + 1) * GLA_VAL_DIM] for h in range(GLA_HEADS)],
                                  axis=0)
            st = st * jnp.exp(b_last[cc * c:cc * c + 1, :]) + _dot_tn(vst, km)
        st_ref[bb] = st

        for h in range(GLA_HEADS):
            vs = slice(h * GLA_VAL_DIM, (h + 1) * GLA_VAL_DIM)
            inter_h = jnp.concatenate([inter[cc][h * c:(h + 1) * c] for cc in range(n_chunk)], axis=0)
            gate = og_ref[bb, :, vs].astype(F32)
            o_ref[bb, :, vs] = (_rms(intra[h] + inter_h, g_ref[:, vs]) * (gate * _sigmoid(gate))).astype(BF16)


def _gla(gq, gk, gv, gog, misc, up_pad, bias, g):
    b, s, _ = gq.shape
    tg = min(GLA_TG, s)
    nb = GLA_NB if b % GLA_NB == 0 else 1

    def per_tile(a):
        return pl.BlockSpec((nb, tg, a.shape[2]), lambda i, j: (i, j, 0))

    def full(a):
        return pl.BlockSpec(a.shape, lambda i, j: (0, 0))

    return pl.pallas_call(
        _gla_kernel,
        grid=(b // nb, s // tg),
        in_specs=[per_tile(gq), per_tile(gk), per_tile(gv), per_tile(gog), per_tile(misc),
                  full(up_pad), full(bias), full(g)],
        out_specs=per_tile(gv),
        out_shape=jax.ShapeDtypeStruct(gv.shape, BF16),
        scratch_shapes=[pltpu.VMEM((nb, GLA_VAL_DIM, GLA_HEADS * GLA_KEY_DIM), F32)],
        compiler_params=_cparams(("parallel", "arbitrary")),
        name="gla",
    )(gq, gk, gv, gog, misc, up_pad, bias, g)


def _overlap_matrix(n_rows):
    n = np.arange(n_rows)[:, None] * CMP_STRIDE
    s = np.arange(LANES)[None, :] * SLC_BLOCK
    return ((n < s + SLC_BLOCK) & (n + CMP_BLOCK > s)).astype(np.float32)


def _layer(h, p, seq, batch, ffn1_pre_g, ffn1_post_g, ffn1_w_gate, ffn1_w_up, ffn1_w_down, mix_pre_g, mix_post_g,
           w_in, cmp_k_pe, cmp_k_w1, cmp_k_w2, cmp_v_pe, cmp_v_w1, cmp_v_w2, nsa_out_g, gla_gate_up, gla_gate_bias,
           gla_out_g, w_out, ffn2_pre_g, ffn2_post_g, ffn2_w_gate, ffn2_w_up, ffn2_w_down, ple_proj, ple_gate,
           ple_post_g):
    d = h.shape[1]
    row = lambda a: a.reshape(1, -1).astype(F32)
    bf = lambda a: a.astype(BF16)

    h1 = _ffn(h, row(ffn1_pre_g), row(ffn1_post_g), bf(ffn1_w_gate), bf(ffn1_w_up), bf(ffn1_w_down))

    o_gate = 512 + 6 * 128
    o_gq = o_gate + 3 * NSA_HEADS
    o_gk = o_gq + GLA_HEADS * GLA_KEY_DIM
    o_gv = o_gk + GLA_HEADS * GLA_KEY_DIM
    o_gd = o_gv + GLA_HEADS * GLA_VAL_DIM
    o_og = o_gd + GLA_GATE_RANK
    pad = jnp.zeros((d, LANES - 3 * NSA_HEADS - GLA_GATE_RANK), w_in.dtype)
    w_all = bf(jnp.concatenate([w_in[:, :o_gate], w_in[:, o_gate:o_gq], w_in[:, o_gd:o_og], pad,
                                w_in[:, o_gq:o_gd], w_in[:, o_og:]], axis=1))
    nq, kc_in, vc_in, ks, vs, kw, vw, misc, gq, gk, gv, gog = _proj(h1, row(mix_pre_g), w_all, seq)

    n_grp = seq // CMP_STRIDE
    grp_w = CMP_STRIDE * NSA_HEAD_DIM
    kc, vc = _compress(
        kc_in.reshape(batch, n_grp, grp_w), vc_in.reshape(batch, n_grp, grp_w),
        cmp_k_pe.reshape(2, grp_w).astype(F32), bf(cmp_k_w1.reshape(2 * grp_w, -1)), bf(cmp_k_w2),
        cmp_v_pe.reshape(2, grp_w).astype(F32), bf(cmp_v_w1.reshape(2 * grp_w, -1)), bf(cmp_v_w2))

    b3 = lambda a: a.reshape(batch, seq, a.shape[-1])
    ovt = jnp.asarray(_overlap_matrix(n_grp).T, BF16)
    o_nsa = _nsa(b3(nq), kc, vc, b3(ks), b3(vs), b3(kw), b3(vw), b3(misc), ovt, row(nsa_out_g))

    up_pad = jnp.zeros((LANES, GLA_HEADS * GLA_KEY_DIM), F32)
    up_pad = up_pad.at[_MISC_GDOWN:_MISC_GDOWN + GLA_GATE_RANK].set(gla_gate_up.astype(F32))
    o_gla = _gla(b3(gq), b3(gk), b3(gv), b3(gog), b3(misc), up_pad, row(gla_gate_bias), row(gla_out_g))

    w_o = bf(w_out)
    half = NSA_HEADS * NSA_HEAD_DIM
    return _ffn(h1, row(ffn2_pre_g), row(ffn2_post_g), bf(ffn2_w_gate), bf(ffn2_w_up), bf(ffn2_w_down),
                mix=(o_nsa.reshape(-1, half), o_gla.reshape(-1, half), w_o[:half], w_o[half:], row(mix_post_g)),
                ple=(p, bf(ple_proj), bf(ple_gate), row(ple_post_g)))


def kernel(x, p, ffn1_pre_g, ffn1_post_g, ffn1_w_gate, ffn1_w_up, ffn1_w_down, mix_pre_g, mix_post_g, w_in, cmp_k_pe, cmp_k_w1, cmp_k_w2, cmp_v_pe, cmp_v_w1, cmp_v_w2, nsa_out_g, gla_gate_up, gla_gate_bias, gla_out_g, w_out, ffn2_pre_g, ffn2_post_g, ffn2_w_gate, ffn2_w_up, ffn2_w_down, ple_proj, ple_gate, ple_post_g):
    batch, seq, d = x.shape
    depth = p.shape[0]
    assert seq % NSA_TK == 0 and seq >= WIN_SLAB and seq // SLC_BLOCK <= LANES
    h = x.reshape(batch * seq, d)
    params = (ffn1_pre_g, ffn1_post_g, ffn1_w_gate, ffn1_w_up, ffn1_w_down, mix_pre_g, mix_post_g, w_in, cmp_k_pe,
              cmp_k_w1, cmp_k_w2, cmp_v_pe, cmp_v_w1, cmp_v_w2, nsa_out_g, gla_gate_up, gla_gate_bias, gla_out_g,
              w_out, ffn2_pre_g, ffn2_post_g, ffn2_w_gate, ffn2_w_up, ffn2_w_down, ple_proj, ple_gate, ple_post_g)
    for i in range(depth):
        h = _layer(h, p[i].reshape(batch * seq, -1), seq, batch, *[a[i] for a in params])
    return h.reshape(batch, seq, d)
```

```python
import functools
import math

import numpy as np
import jax
import jax.numpy as jnp
from jax import lax
from jax.experimental import pallas as pl
from jax.experimental.pallas import tpu as pltpu

F32 = jnp.float32
BF16 = jnp.bfloat16

EPS = 1e-6
NSA_HEADS = 4
NSA_HEAD_DIM = 128
CMP_BLOCK = 32
CMP_STRIDE = 16
SLC_BLOCK = 64
SLC_TOPK = 16
WINDOW = 512
FORCE_BONUS = 1e4
NEG = -1e30
GLA_HEADS = 4
GLA_KEY_DIM = 64
GLA_VAL_DIM = 128
GLA_GATE_RANK = 16
GLA_GATE_NORM = 16.0
GLA_CHUNK = 64

LOG2E = 1.4426950408889634
N_FORCED = 3
N_BRANCH = 3

LANES = 128
VMEM_LIMIT = 56 * 1024 * 1024

FFN_TM = 1024
FFN_TF = 256
TOK_TM = 512
NSA_TQ = 128
NSA_SUB = 4
NSA_TK = 512
NSA_UNROLL = 4
GLA_TG = 256
GLA_NB = 4
WIN_SLAB = WINDOW + NSA_TQ


def _dot(a, b):
    return jnp.dot(a, b, preferred_element_type=F32)


def _dot_nt(a, b):
    return lax.dot_general(a, b, (((1,), (1,)), ((), ())), preferred_element_type=F32)


def _dot_tn(a, b):
    return lax.dot_general(a, b, (((0,), (0,)), ((), ())), preferred_element_type=F32)


def _split(x):
    hi = x.astype(BF16)
    lo = (x - hi.astype(F32)).astype(BF16)
    return hi, lo


def _rms(x, g):
    return x * lax.rsqrt(jnp.mean(x * x, axis=-1, keepdims=True) + EPS) * g


def _sigmoid(x):
    return 1.0 / (1.0 + jnp.exp(-x))


def _cparams(sem):
    return pltpu.CompilerParams(dimension_semantics=sem, vmem_limit_bytes=VMEM_LIMIT)


def _trace_skewed(stage_generators):
    pending = list(stage_generators)
    running = []
    while running or pending:
        if pending:
            running.append(pending.pop(0))
        for gen in list(running):
            if next(gen, "done") == "done":
                running.remove(gen)


def _ffn_kernel(n_mix, n_ple, *refs):
    h_ref, pre_ref, post_ref, wg_ref, wu_ref, wd_ref = refs[:6]
    mix_refs = refs[6:6 + n_mix]
    ple_refs = refs[6 + n_mix:6 + n_mix + n_ple]
    o_ref = refs[-1]
    h = h_ref[...]
    if n_mix:
        on_ref, ogl_ref, wa_ref, wb_ref, mg_ref = mix_refs
        h = h + _rms(_dot(on_ref[...], wa_ref[...]) + _dot(ogl_ref[...], wb_ref[...]), mg_ref[...])
    u = _rms(h, pre_ref[...]).astype(BF16)
    acc = None
    for c in range(wg_ref.shape[1] // FFN_TF):
        sl = slice(c * FFN_TF, (c + 1) * FFN_TF)
        a = _dot(u, wg_ref[:, sl])
        b = _dot(u, wu_ref[:, sl])
        part = _dot((a * _sigmoid(a) * b).astype(BF16), wd_ref[sl, :])
        acc = part if acc is None else acc + part
    h = h + 0.5 * _rms(acc, post_ref[...])
    if n_ple:
        p_ref, wp_ref, wgate_ref, pg_ref = ple_refs
        e = _dot(p_ref[...].astype(BF16), wp_ref[...])
        h = h + _rms(e * _sigmoid(_dot(h.astype(BF16), wgate_ref[...])), pg_ref[...])
    o_ref[...] = h


def _resident(a):
    return pl.BlockSpec(a.shape, lambda *_: (0,) * a.ndim, pipeline_mode=pl.Buffered(1))


def _ffn(h, pre_g, post_g, wg, wu, wd, mix=(), ple=()):
    t, d = h.shape
    tm = min(FFN_TM, t)

    def tile(a):
        return pl.BlockSpec((tm, a.shape[1]), lambda i: (i, 0))

    args = [h, pre_g, post_g, wg, wu, wd, *mix, *ple]
    specs = [tile(h)] + [_resident(a) for a in (pre_g, post_g, wg, wu, wd)]
    if mix:
        specs += [tile(mix[0]), tile(mix[1])] + [_resident(a) for a in mix[2:]]
    if ple:
        specs += [tile(ple[0])] + [_resident(a) for a in ple[1:]]
    return pl.pallas_call(
        functools.partial(_ffn_kernel, len(mix), len(ple)),
        grid=(t // tm,),
        in_specs=specs,
        out_specs=tile(h),
        out_shape=jax.ShapeDtypeStruct((t, d), F32),
        compiler_params=_cparams(("parallel",)),
        name="ffn",
    )(*args)


_P_NQ = (0, 512)
_P_KC = (512, 640)
_P_VC = (640, 768)
_P_KS = (768, 896)
_P_VS = (896, 1024)
_P_KW = (1024, 1152)
_P_VW = (1152, 1280)
_P_MISC = (1280, 1408)
_P_GQ = (1408, 1664)
_P_GK = (1664, 1920)
_P_GV = (1920, 2432)
_P_GOG = (2432, 2944)
_P_WIDTH = 2944
_MISC_GDOWN = 3 * NSA_HEADS


def _proj_kernel(seq, h_ref, g_ref, w_ref, nq_ref, kc_ref, vc_ref, ks_ref, vs_ref, kw_ref, vw_ref,
                 misc_ref, gq_ref, gk_ref, gv_ref, gog_ref):
    tm = h_ref.shape[0]
    u = _rms(h_ref[...], g_ref[...]).astype(BF16)

    def grp(lohi):
        return _dot(u, w_ref[:, lohi[0]:lohi[1]])

    nq_ref[...] = (grp(_P_NQ) * (NSA_HEAD_DIM ** -0.5 * LOG2E)).astype(BF16)
    kv_c = grp((_P_KC[0], _P_VC[1])).astype(BF16)
    kc_ref[...] = kv_c[:, 0:LANES]
    vc_ref[...] = kv_c[:, LANES:2 * LANES]
    base = (pl.program_id(0) % (seq // tm)) * tm
    row = lax.broadcasted_iota(jnp.int32, (tm, LANES), 0)
    lane = lax.broadcasted_iota(jnp.int32, (tm, LANES), 1)
    onehot = jnp.where(lane == ((base + row) >> int(math.log2(SLC_BLOCK))), 1.0, 0.0).astype(BF16)
    kv_s = grp((_P_KS[0], _P_VS[1])).astype(BF16)
    ks_ref[:, 0:LANES] = kv_s[:, 0:LANES]
    ks_ref[:, LANES:2 * LANES] = onehot
    vs_ref[...] = kv_s[:, LANES:2 * LANES]
    kv_w = grp((_P_KW[0], _P_VW[1])).astype(BF16)
    kw_ref[...] = kv_w[:, 0:LANES]
    vw_ref[...] = kv_w[:, LANES:2 * LANES]
    misc_ref[...] = grp(_P_MISC)
    gq_ref[...] = grp(_P_GQ).astype(BF16)
    gk_ref[...] = grp(_P_GK).astype(BF16)
    gv_ref[...] = grp(_P_GV).astype(BF16)
    gog_ref[...] = grp(_P_GOG).astype(BF16)


def _proj(h, g, w_all, seq):
    t, d = h.shape
    tm = min(TOK_TM, seq)
    widths = [(512, BF16), (128, BF16), (128, BF16), (256, BF16), (128, BF16), (128, BF16), (128, BF16),
              (128, F32), (256, BF16), (256, BF16), (512, BF16), (512, BF16)]
    return pl.pallas_call(
        functools.partial(_proj_kernel, seq),
        grid=(t // tm,),
        in_specs=[
            pl.BlockSpec((tm, d), lambda i: (i, 0)),
            pl.BlockSpec((1, d), lambda i: (0, 0)),
            pl.BlockSpec((d, _P_WIDTH), lambda i: (0, 0)),
        ],
        out_specs=[pl.BlockSpec((tm, w), lambda i: (i, 0)) for w, _ in widths],
        out_shape=[jax.ShapeDtypeStruct((t, w), dt) for w, dt in widths],
        compiler_params=_cparams(("parallel",)),
        name="proj",
    )(h, g, w_all)


def _compress_kernel(k_ref, v_ref, pek_ref, w1k_ref, w2k_ref, pev_ref, w1v_ref, w2v_ref, kc_ref, vc_ref):
    half = CMP_STRIDE * NSA_HEAD_DIM

    def one(x_ref, pe_ref, w1_ref, w2_ref):
        x = x_ref[...].astype(F32)
        n = x.shape[0]
        a = _dot((x + pe_ref[0:1, :]).astype(BF16), w1_ref[0:half, :])
        b = _dot((x + pe_ref[1:2, :]).astype(BF16), w1_ref[half:2 * half, :])
        hid = a + pltpu.roll(b, n - 1, 0)
        act = (hid * _sigmoid(hid)).astype(BF16)
        return _dot(act, w2_ref[...])

    kc_ref[...] = one(k_ref, pek_ref, w1k_ref, w2k_ref).astype(BF16)
    vc_ref[...] = one(v_ref, pev_ref, w1v_ref, w2v_ref).T.astype(BF16)


def _compress(k2, v2, pek, w1k, w2k, pev, w1v, w2v):
    b, n, w = k2.shape
    dk = w2k.shape[1]
    big = pl.BlockSpec((None, n, w), lambda i: (i, 0, 0))

    def full(a):
        return pl.BlockSpec(a.shape, lambda i: (0,) * a.ndim)

    return pl.pallas_call(
        _compress_kernel,
        grid=(b,),
        in_specs=[big, big, full(pek), full(w1k), full(w2k), full(pev), full(w1v), full(w2v)],
        out_specs=[pl.BlockSpec((None, n, dk), lambda i: (i, 0, 0)), pl.BlockSpec((None, dk, n), lambda i: (i, 0, 0))],
        out_shape=[jax.ShapeDtypeStruct((b, n, dk), BF16), jax.ShapeDtypeStruct((b, dk, n), BF16)],
        compiler_params=_cparams(("parallel",)),
        name="compress",
    )(k2, v2, pek, w1k, w2k, pev, w1v, w2v)


def _nsa_kernel(q_ref, kc_ref, vct_ref, ks_ref, vs_ref, kw_ref, vw_ref, misc_ref, ovt_ref, og_ref, o_ref,
                qaug_ref, m_ref, acc_ref, oc_ref, gate_ref):
    tq = NSA_TQ
    rows = NSA_HEADS * tq
    jd = pl.program_id(1)
    step0 = jd * NSA_TK
    n_c = kc_ref.shape[0]
    log_g = int(math.log2(CMP_STRIDE))
    row_i = lax.broadcasted_iota(jnp.int32, (rows, LANES), 0) & (tq - 1)
    lane = lax.broadcasted_iota(jnp.int32, (rows, LANES), 1)
    lane_q = lax.broadcasted_iota(jnp.int32, (1, rows), 1) & (tq - 1)
    tri_diag = jnp.where(lane <= row_i, 0.0, NEG)
    tri_far = jnp.where(lane > row_i, 0.0, NEG)
    q_grp = jnp.where(lane == ((row_i + 1) >> log_g), 1.0, 0.0).astype(BF16)
    n_minus_g = (lax.broadcasted_iota(jnp.int32, (n_c, LANES), 0)
                 - lax.broadcasted_iota(jnp.int32, (n_c, LANES), 1))

    n_gate = N_BRANCH * NSA_HEADS
    spread = jnp.where(lax.broadcasted_iota(jnp.int32, (LANES, n_gate * LANES), 0)
                       == (lax.broadcasted_iota(jnp.int32, (LANES, n_gate * LANES), 1) >> int(math.log2(LANES))),
                       1.0, 0.0).astype(BF16)
    g_hi, g_lo = _split(_sigmoid(misc_ref[...]))
    gate_ref[...] = _dot(g_hi, spread) + _dot(g_lo, spread)

    def heads_to_rows(q):
        return jnp.concatenate([q[:, h * NSA_HEAD_DIM:(h + 1) * NSA_HEAD_DIM] for h in range(NSA_HEADS)], axis=0)

    def prepare(r):
        q0 = step0 + r * tq
        qh = heads_to_rows(q_ref[r * tq:(r + 1) * tq, :])
        vis = jnp.where(n_minus_g <= (q0 >> log_g) - 2, 0.0, NEG).astype(BF16)
        st = _dot_nt(jnp.concatenate([kc_ref[...], vis], axis=1), jnp.concatenate([qh, q_grp], axis=1))
        yield
        et = jnp.exp2(st - jnp.max(st, axis=0, keepdims=True))
        inv = jnp.where(q0 + lane_q >= CMP_BLOCK - 1, 1.0 / jnp.sum(et, axis=0, keepdims=True), 0.0)
        p_t = et * inv
        p_bf = p_t.astype(BF16)
        p_sum = p_t[:, 0:tq]
        for h in range(1, NSA_HEADS):
            p_sum = p_sum + p_t[:, h * tq:(h + 1) * tq]
        p_hi, p_lo = _split(p_sum)
        yield
        oc_ref[r] = _dot(vct_ref[...], p_bf).T
        imp = _dot(ovt_ref[...], p_hi) + _dot(ovt_ref[...], p_lo)
        yield
        blk = lax.broadcasted_iota(jnp.int32, (LANES, tq), 0)
        cur = (q0 + lax.broadcasted_iota(jnp.int32, (LANES, tq), 1)) >> int(math.log2(SLC_BLOCK))
        forced = (blk == 0) | (blk == cur) | (blk == cur - 1)
        val = jnp.where(forced | (blk > cur), -1.0, imp)
        blk_f = blk.astype(F32)
        bias_t = jnp.where(forced, 0.0, NEG)
        for _ in range(SLC_TOPK - N_FORCED):
            top = jnp.max(val, axis=0, keepdims=True)
            first = jnp.min(jnp.where(val == top, blk_f, float(LANES)), axis=0, keepdims=True)
            pick = blk_f == first
            bias_t = jnp.where(pick, 0.0, bias_t)
            val = jnp.where(pick, -2.0, val)
        bias = bias_t.T.astype(BF16)
        qaug_ref[r] = jnp.concatenate([qh, jnp.concatenate([bias] * NSA_HEADS, axis=0)], axis=1)
        m_ref[r] = jnp.full((rows, LANES), NEG, F32)
        acc_ref[r] = jnp.zeros((rows, 2 * NSA_HEAD_DIM), F32)
        yield

    _trace_skewed(prepare(r) for r in range(NSA_SUB))

    def mask_last_chunk(sc):
        w = sc.shape[1]
        if w == LANES:
            return sc + tri_diag
        return jnp.concatenate([sc[:, :w - LANES], sc[:, w - LANES:] + tri_diag], axis=1)

    def with_ones(v):
        return jnp.concatenate([v, jnp.ones(v.shape, BF16)], axis=1)

    def flash(r, k0):
        sc = _dot_nt(qaug_ref[r], ks_ref[pl.ds(k0, NSA_TK), :])
        yield
        m_i = m_ref[r]
        m_n = jnp.maximum(m_i, jnp.max(sc, axis=1, keepdims=True))
        alpha = jnp.exp2(m_i - m_n)
        p = jnp.exp2(sc - jnp.concatenate([m_n] * (NSA_TK // LANES), axis=1)).astype(BF16)
        yield
        acc_ref[r] = (jnp.concatenate([alpha, alpha], axis=1) * acc_ref[r]
                      + _dot(p, with_ones(vs_ref[pl.ds(k0, NSA_TK), :])))
        m_ref[r] = m_n
        yield

    def full_tiles(j0, n):
        _trace_skewed(flash(r, pl.multiple_of((j0 + t) * NSA_TK, NSA_TK))
                      for t in range(n) for r in range(NSA_SUB))

    def tile_group(j, carry):
        full_tiles(j * NSA_UNROLL, NSA_UNROLL)
        return carry

    lax.fori_loop(0, jd // NSA_UNROLL, tile_group, 0)
    for rem in range(1, NSA_UNROLL):
        @pl.when(jd % NSA_UNROLL == rem)
        def _(rem=rem):
            full_tiles(jd - rem, rem)

    def finish(r, first_step):
        w = (r + 1) * tq
        d0 = pl.multiple_of(step0, NSA_TK)
        sc = mask_last_chunk(_dot_nt(qaug_ref[r], ks_ref[pl.ds(d0, w), :]))
        vt = vs_ref[pl.ds(d0, w), :]
        qh = qaug_ref[r][:, 0:NSA_HEAD_DIM]
        if first_step:
            sw = mask_last_chunk(_dot_nt(qh, kw_ref[0:w, :]))
            vw = vw_ref[0:w, :]
        else:
            w0 = pl.multiple_of(step0 + r * tq - WINDOW, tq)
            sw = _dot_nt(qh, kw_ref[pl.ds(w0, WIN_SLAB), :])
            sw = jnp.concatenate([sw[:, :LANES] + tri_far, sw[:, LANES:WIN_SLAB - LANES],
                                  sw[:, WIN_SLAB - LANES:] + tri_diag], axis=1)
            vw = vw_ref[pl.ds(w0, WIN_SLAB), :]
        yield
        m_i = m_ref[r]
        m_n = jnp.maximum(m_i, jnp.max(sc, axis=1, keepdims=True))
        alpha = jnp.exp2(m_i - m_n)
        p = jnp.exp2(sc - jnp.concatenate([m_n] * (w // LANES), axis=1)).astype(BF16)
        ew = jnp.exp2(sw - jnp.max(sw, axis=1, keepdims=True)).astype(BF16)
        yield
        acc = jnp.concatenate([alpha, alpha], axis=1) * acc_ref[r] + _dot(p, with_ones(vt))
        pv = _dot(ew, with_ones(vw))
        yield
        o_s = acc[:, 0:NSA_HEAD_DIM] * (1.0 / acc[:, NSA_HEAD_DIM:])
        o_w = pv[:, 0:NSA_HEAD_DIM] * (1.0 / pv[:, NSA_HEAD_DIM:])
        o_c = oc_ref[r]
        outs = []
        for h in range(NSA_HEADS):
            hr = slice(h * tq, (h + 1) * tq)
            gc, gs, gw = (gate_ref[r * tq:(r + 1) * tq,
                                   (N_BRANCH * h + k) * LANES:(N_BRANCH * h + k + 1) * LANES]
                          for k in range(N_BRANCH))
            o_h = gc * o_c[hr] + gs * o_s[hr] + gw * o_w[hr]
            outs.append(_rms(o_h, og_ref[:, h * NSA_HEAD_DIM:(h + 1) * NSA_HEAD_DIM]))
        o_ref[r * tq:(r + 1) * tq, :] = jnp.concatenate(outs, axis=1).astype(BF16)
        yield

    @pl.when(jd == 0)
    def _():
        _trace_skewed(finish(r, True) for r in range(NSA_SUB))

    @pl.when(jd > 0)
    def _():
        _trace_skewed(finish(r, False) for r in range(NSA_SUB))


def _nsa(nq, kc, vct, ks, vs, kw, vw, misc, ovt, og):
    b, s, wq = nq.shape
    ts = NSA_SUB * NSA_TQ
    rows = NSA_HEADS * NSA_TQ

    def per_batch(a):
        return pl.BlockSpec((None,) + a.shape[1:], lambda i, j: (i, 0, 0))

    def per_tile(a):
        return pl.BlockSpec((None, ts, a.shape[2]), lambda i, j: (i, j, 0))

    def full(a):
        return pl.BlockSpec(a.shape, lambda i, j: (0, 0))

    return pl.pallas_call(
        _nsa_kernel,
        grid=(b, s // ts),
        in_specs=[per_tile(nq), per_batch(kc), per_batch(vct), per_batch(ks), per_batch(vs), per_batch(kw),
                  per_batch(vw), per_tile(misc), full(ovt), full(og)],
        out_specs=pl.BlockSpec((None, ts, wq), lambda i, j: (i, j, 0)),
        out_shape=jax.ShapeDtypeStruct((b, s, wq), BF16),
        scratch_shapes=[pltpu.VMEM((NSA_SUB, rows, 2 * NSA_HEAD_DIM), BF16),
                        pltpu.VMEM((NSA_SUB, rows, LANES), F32),
                        pltpu.VMEM((NSA_SUB, rows, 2 * NSA_HEAD_DIM), F32),
                        pltpu.VMEM((NSA_SUB, rows, NSA_HEAD_DIM), F32),
                        pltpu.VMEM((ts, N_BRANCH * NSA_HEADS * LANES), F32)],
        compiler_params=_cparams(("parallel", "arbitrary")),
        name="nsa",
    )(nq, kc, vct, ks, vs, kw, vw, misc, ovt, og)


def _gla_kernel(q_ref, k_ref, v_ref, og_ref, misc_ref, up_ref, bias_ref, g_ref, o_ref, st_ref):
    c = GLA_CHUNK
    kw = GLA_HEADS * GLA_KEY_DIM
    tg = q_ref.shape[1]
    n_chunk = tg // c
    log_c = int(math.log2(c))

    @pl.when(pl.program_id(1) == 0)
    def _():
        st_ref[...] = jnp.zeros_like(st_ref)

    ri = lax.broadcasted_iota(jnp.int32, (tg, tg), 0)
    ci = lax.broadcasted_iota(jnp.int32, (tg, tg), 1)
    low = ((ri >> log_c) == (ci >> log_c)) & (ci <= ri)
    causal = jnp.where(low, 1.0, 0.0)
    tri = causal.astype(BF16)
    lane_head = lax.broadcasted_iota(jnp.int32, (1, kw), 1) >> int(math.log2(GLA_KEY_DIM))
    head_mask = [lane_head == h for h in range(GLA_HEADS)]
    u_hi, u_lo = _split(up_ref[...])

    def stack_heads(x):
        return jnp.concatenate([jnp.where(head_mask[h], x, 0.0) for h in range(GLA_HEADS)], axis=0)

    def sequence(bb):
        m_hi, m_lo = _split(misc_ref[bb])
        g_pre = _dot(m_hi, u_hi) + _dot(m_lo, u_hi) + _dot(m_hi, u_lo) + bias_ref[...]
        yield
        g_log = (jnp.minimum(g_pre, 0.0) - jnp.log(1.0 + jnp.exp(-jnp.abs(g_pre)))) * (1.0 / GLA_GATE_NORM)
        g_hi, g_lo = _split(g_log)
        yield
        bcum = _dot(tri, g_hi) + _dot(tri, g_lo)
        yield
        b_last = jnp.concatenate([jnp.broadcast_to(bcum[(cc + 1) * c - 1:(cc + 1) * c, :], (c, kw))
                                  for cc in range(n_chunk)], axis=0)
        qf = q_ref[bb].astype(F32)
        kf = k_ref[bb].astype(F32)
        v = v_ref[bb]
        q_dec = qf * jnp.exp(bcum) * (GLA_KEY_DIM ** -0.5)
        k_dec = (kf * jnp.exp(-bcum)).astype(BF16)
        k_end = kf * jnp.exp(b_last - bcum)
        q_heads = [jnp.where(head_mask[h], q_dec, 0.0).astype(BF16) for h in range(GLA_HEADS)]
        yield
        scores = [_dot_nt(q_heads[h], k_dec) for h in range(GLA_HEADS)]
        yield
        a = [(scores[h] * causal).astype(BF16) for h in range(GLA_HEADS)]
        yield
        intra = [_dot(a[h], v[:, h * GLA_VAL_DIM:(h + 1) * GLA_VAL_DIM]) for h in range(GLA_HEADS)]
        st = st_ref[bb]
        inter = []
        for cc in range(n_chunk):
            r = slice(cc * c, (cc + 1) * c)
            inter.append(_dot_nt(stack_heads(q_dec[r]).astype(BF16), st.astype(BF16)))
            km = stack_heads(k_end[r]).astype(BF16)
            vst = jnp.concatenate([v[r, h * GLA_VAL_DIM:(h + 1) * GLA_VAL_DIM] for h in range(GLA_HEADS)],
                                  axis=0)
            st = st * jnp.exp(b_last[cc * c:cc * c + 1, :]) + _dot_tn(vst, km)
        st_ref[bb] = st
        yield
        for h in range(GLA_HEADS):
            vs = slice(h * GLA_VAL_DIM, (h + 1) * GLA_VAL_DIM)
            inter_h = jnp.concatenate([inter[cc][h * c:(h + 1) * c] for cc in range(n_chunk)], axis=0)
            gate = og_ref[bb, :, vs].astype(F32)
            o_ref[bb, :, vs] = (_rms(intra[h] + inter_h, g_ref[:, vs]) * (gate * _sigmoid(gate))).astype(BF16)
        yield

    _trace_skewed(sequence(bb) for bb in range(q_ref.shape[0]))


def _gla(gq, gk, gv, gog, misc, up_pad, bias, g):
    b, s, _ = gq.shape
    tg = min(GLA_TG, s)
    nb = GLA_NB if b % GLA_NB == 0 else 1

    def per_tile(a):
        return pl.BlockSpec((nb, tg, a.shape[2]), lambda i, j: (i, j, 0))

    def full(a):
        return pl.BlockSpec(a.shape, lambda i, j: (0, 0))

    return pl.pallas_call(
        _gla_kernel,
        grid=(b // nb, s // tg),
        in_specs=[per_tile(gq), per_tile(gk), per_tile(gv), per_tile(gog), per_tile(misc),
                  full(up_pad), full(bias), full(g)],
        out_specs=per_tile(gv),
        out_shape=jax.ShapeDtypeStruct(gv.shape, BF16),
        scratch_shapes=[pltpu.VMEM((nb, GLA_VAL_DIM, GLA_HEADS * GLA_KEY_DIM), F32)],
        compiler_params=_cparams(("parallel", "arbitrary")),
        name="gla",
    )(gq, gk, gv, gog, misc, up_pad, bias, g)


def _overlap_matrix(n_rows):
    n = np.arange(n_rows)[:, None] * CMP_STRIDE
    s = np.arange(LANES)[None, :] * SLC_BLOCK
    return ((n < s + SLC_BLOCK) & (n + CMP_BLOCK > s)).astype(np.float32)


def _layer(h, p, seq, batch, ffn1_pre_g, ffn1_post_g, ffn1_w_gate, ffn1_w_up, ffn1_w_down, mix_pre_g, mix_post_g,
           w_in, cmp_k_pe, cmp_k_w1, cmp_k_w2, cmp_v_pe, cmp_v_w1, cmp_v_w2, nsa_out_g, gla_gate_up, gla_gate_bias,
           gla_out_g, w_out, ffn2_pre_g, ffn2_post_g, ffn2_w_gate, ffn2_w_up, ffn2_w_down, ple_proj, ple_gate,
           ple_post_g):
    d = h.shape[1]
    row = lambda a: a.reshape(1, -1).astype(F32)
    bf = lambda a: a.astype(BF16)

    h1 = _ffn(h, row(ffn1_pre_g), row(ffn1_post_g), bf(ffn1_w_gate), bf(ffn1_w_up), bf(ffn1_w_down))

    o_gate = 512 + 6 * 128
    o_gq = o_gate + 3 * NSA_HEADS
    o_gk = o_gq + GLA_HEADS * GLA_KEY_DIM
    o_gv = o_gk + GLA_HEADS * GLA_KEY_DIM
    o_gd = o_gv + GLA_HEADS * GLA_VAL_DIM
    o_og = o_gd + GLA_GATE_RANK
    pad = jnp.zeros((d, LANES - 3 * NSA_HEADS - GLA_GATE_RANK), w_in.dtype)
    w_all = bf(jnp.concatenate([w_in[:, :o_gate], w_in[:, o_gate:o_gq], w_in[:, o_gd:o_og], pad,
                                w_in[:, o_gq:o_gd], w_in[:, o_og:]], axis=1))
    nq, kc_in, vc_in, ks, vs, kw, vw, misc, gq, gk, gv, gog = _proj(h1, row(mix_pre_g), w_all, seq)

    n_grp = seq // CMP_STRIDE
    grp_w = CMP_STRIDE * NSA_HEAD_DIM
    kc, vc = _compress(
        kc_in.reshape(batch, n_grp, grp_w), vc_in.reshape(batch, n_grp, grp_w),
        cmp_k_pe.reshape(2, grp_w).astype(F32), bf(cmp_k_w1.reshape(2 * grp_w, -1)), bf(cmp_k_w2),
        cmp_v_pe.reshape(2, grp_w).astype(F32), bf(cmp_v_w1.reshape(2 * grp_w, -1)), bf(cmp_v_w2))

    b3 = lambda a: a.reshape(batch, seq, a.shape[-1])
    ovt = jnp.asarray(_overlap_matrix(n_grp).T, BF16)
    o_nsa = _nsa(b3(nq), kc, vc, b3(ks), b3(vs), b3(kw), b3(vw), b3(misc), ovt, row(nsa_out_g))

    up_pad = jnp.zeros((LANES, GLA_HEADS * GLA_KEY_DIM), F32)
    up_pad = up_pad.at[_MISC_GDOWN:_MISC_GDOWN + GLA_GATE_RANK].set(gla_gate_up.astype(F32))
    o_gla = _gla(b3(gq), b3(gk), b3(gv), b3(gog), b3(misc), up_pad, row(gla_gate_bias), row(gla_out_g))

    w_o = bf(w_out)
    half = NSA_HEADS * NSA_HEAD_DIM
    return _ffn(h1, row(ffn2_pre_g), row(ffn2_post_g), bf(ffn2_w_gate), bf(ffn2_w_up), bf(ffn2_w_down),
                mix=(o_nsa.reshape(-1, half), o_gla.reshape(-1, half), w_o[:half], w_o[half:], row(mix_post_g)),
                ple=(p, bf(ple_proj), bf(ple_gate), row(ple_post_g)))


def kernel(x, p, ffn1_pre_g, ffn1_post_g, ffn1_w_gate, ffn1_w_up, ffn1_w_down, mix_pre_g, mix_post_g, w_in, cmp_k_pe, cmp_k_w1, cmp_k_w2, cmp_v_pe, cmp_v_w1, cmp_v_w2, nsa_out_g, gla_gate_up, gla_gate_bias, gla_out_g, w_out, ffn2_pre_g, ffn2_post_g, ffn2_w_gate, ffn2_w_up, ffn2_w_down, ple_proj, ple_gate, ple_post_g):
    batch, seq, d = x.shape
    depth = p.shape[0]
    assert seq % NSA_TK == 0 and seq >= WIN_SLAB and seq // SLC_BLOCK <= LANES
    h = x.reshape(batch * seq, d)
    params = (ffn1_pre_g, ffn1_post_g, ffn1_w_gate, ffn1_w_up, ffn1_w_down, mix_pre_g, mix_post_g, w_in, cmp_k_pe,
              cmp_k_w1, cmp_k_w2, cmp_v_pe, cmp_v_w1, cmp_v_w2, nsa_out_g, gla_gate_up, gla_gate_bias, gla_out_g,
              w_out, ffn2_pre_g, ffn2_post_g, ffn2_w_gate, ffn2_w_up, ffn2_w_down, ple_proj, ple_gate, ple_post_g)
    for i in range(depth):
        h = _layer(h, p[i].reshape(batch * seq, -1), seq, batch, *[a[i] for a in params])
    return h.reshape(batch, seq, d)
```

```python
import functools
import math

import numpy as np
import jax
import jax.numpy as jnp
from jax import lax
from jax.experimental import pallas as pl
from jax.experimental.pallas import tpu as pltpu

F32 = jnp.float32
BF16 = jnp.bfloat16

EPS = 1e-6
NSA_HEADS = 4
NSA_HEAD_DIM = 128
CMP_BLOCK = 32
CMP_STRIDE = 16
SLC_BLOCK = 64
SLC_TOPK = 16
WINDOW = 512
FORCE_BONUS = 1e4
NEG = -1e30
GLA_HEADS = 4
GLA_KEY_DIM = 64
GLA_VAL_DIM = 128
GLA_GATE_RANK = 16
GLA_GATE_NORM = 16.0
GLA_CHUNK = 64

LOG2E = 1.4426950408889634
N_FORCED = 3
N_BRANCH = 3

LANES = 128
VMEM_LIMIT = 56 * 1024 * 1024

FFN_TM = 1024
FFN_TF = 256
TOK_TM = 512
NSA_TQ = 128
NSA_SUB = 4
NSA_TK = 512
NSA_UNROLL = 4
CMP_BUCKET = 128
GLA_TG = 256
GLA_NB = 4
WIN_SLAB = WINDOW + NSA_TQ


def _dot(a, b):
    return jnp.dot(a, b, preferred_element_type=F32)


def _dot_nt(a, b):
    return lax.dot_general(a, b, (((1,), (1,)), ((), ())), preferred_element_type=F32)


def _dot_tn(a, b):
    return lax.dot_general(a, b, (((0,), (0,)), ((), ())), preferred_element_type=F32)


def _split(x):
    hi = x.astype(BF16)
    lo = (x - hi.astype(F32)).astype(BF16)
    return hi, lo


def _rms(x, g):
    return x * lax.rsqrt(jnp.mean(x * x, axis=-1, keepdims=True) + EPS) * g


def _sigmoid(x):
    return 1.0 / (1.0 + jnp.exp(-x))


def _cparams(sem):
    return pltpu.CompilerParams(dimension_semantics=sem, vmem_limit_bytes=VMEM_LIMIT)


def _trace_skewed(stage_generators):
    pending = list(stage_generators)
    running = []
    while running or pending:
        if pending:
            running.append(pending.pop(0))
        for gen in list(running):
            if next(gen, "done") == "done":
                running.remove(gen)


def _ffn_kernel(n_mix, n_ple, *refs):
    h_ref, pre_ref, post_ref, wg_ref, wu_ref, wd_ref = refs[:6]
    mix_refs = refs[6:6 + n_mix]
    ple_refs = refs[6 + n_mix:6 + n_mix + n_ple]
    o_ref = refs[-1]
    h = h_ref[...]
    if n_mix:
        on_ref, ogl_ref, wa_ref, wb_ref, mg_ref = mix_refs
        h = h + _rms(_dot(on_ref[...], wa_ref[...]) + _dot(ogl_ref[...], wb_ref[...]), mg_ref[...])
    u = _rms(h, pre_ref[...]).astype(BF16)
    acc = None
    for c in range(wg_ref.shape[1] // FFN_TF):
        sl = slice(c * FFN_TF, (c + 1) * FFN_TF)
        a = _dot(u, wg_ref[:, sl])
        b = _dot(u, wu_ref[:, sl])
        part = _dot((a * _sigmoid(a) * b).astype(BF16), wd_ref[sl, :])
        acc = part if acc is None else acc + part
    h = h + 0.5 * _rms(acc, post_ref[...])
    if n_ple:
        p_ref, wp_ref, wgate_ref, pg_ref = ple_refs
        e = _dot(p_ref[...].astype(BF16), wp_ref[...])
        h = h + _rms(e * _sigmoid(_dot(h.astype(BF16), wgate_ref[...])), pg_ref[...])
    o_ref[...] = h


def _resident(a):
    return pl.BlockSpec(a.shape, lambda *_: (0,) * a.ndim, pipeline_mode=pl.Buffered(1))


def _ffn(h, pre_g, post_g, wg, wu, wd, mix=(), ple=()):
    t, d = h.shape
    tm = min(FFN_TM, t)

    def tile(a):
        return pl.BlockSpec((tm, a.shape[1]), lambda i: (i, 0))

    args = [h, pre_g, post_g, wg, wu, wd, *mix, *ple]
    specs = [tile(h)] + [_resident(a) for a in (pre_g, post_g, wg, wu, wd)]
    if mix:
        specs += [tile(mix[0]), tile(mix[1])] + [_resident(a) for a in mix[2:]]
    if ple:
        specs += [tile(ple[0])] + [_resident(a) for a in ple[1:]]
    return pl.pallas_call(
        functools.partial(_ffn_kernel, len(mix), len(ple)),
        grid=(t // tm,),
        in_specs=specs,
        out_specs=tile(h),
        out_shape=jax.ShapeDtypeStruct((t, d), F32),
        compiler_params=_cparams(("parallel",)),
        name="ffn",
    )(*args)


_P_NQ = (0, 512)
_P_KC = (512, 640)
_P_VC = (640, 768)
_P_KS = (768, 896)
_P_VS = (896, 1024)
_P_KW = (1024, 1152)
_P_VW = (1152, 1280)
_P_MISC = (1280, 1408)
_P_GQ = (1408, 1664)
_P_GK = (1664, 1920)
_P_GV = (1920, 2432)
_P_GOG = (2432, 2944)
_P_WIDTH = 2944
_MISC_GDOWN = 3 * NSA_HEADS


def _proj_kernel(seq, h_ref, g_ref, w_ref, nq_ref, kc_ref, vc_ref, ks_ref, vs_ref, kw_ref, vw_ref,
                 misc_ref, gq_ref, gk_ref, gv_ref, gog_ref):
    tm = h_ref.shape[0]
    u = _rms(h_ref[...], g_ref[...]).astype(BF16)

    def grp(lohi):
        return _dot(u, w_ref[:, lohi[0]:lohi[1]])

    nq_ref[...] = (grp(_P_NQ) * (NSA_HEAD_DIM ** -0.5 * LOG2E)).astype(BF16)
    kv_c = grp((_P_KC[0], _P_VC[1])).astype(BF16)
    kc_ref[...] = kv_c[:, 0:LANES]
    vc_ref[...] = kv_c[:, LANES:2 * LANES]
    base = (pl.program_id(0) % (seq // tm)) * tm
    row = lax.broadcasted_iota(jnp.int32, (tm, LANES), 0)
    lane = lax.broadcasted_iota(jnp.int32, (tm, LANES), 1)
    onehot = jnp.where(lane == ((base + row) >> int(math.log2(SLC_BLOCK))), 1.0, 0.0).astype(BF16)
    kv_s = grp((_P_KS[0], _P_VS[1])).astype(BF16)
    ks_ref[:, 0:LANES] = kv_s[:, 0:LANES]
    ks_ref[:, LANES:2 * LANES] = onehot
    vs_ref[...] = kv_s[:, LANES:2 * LANES]
    kv_w = grp((_P_KW[0], _P_VW[1])).astype(BF16)
    kw_ref[...] = kv_w[:, 0:LANES]
    vw_ref[...] = kv_w[:, LANES:2 * LANES]
    misc_ref[...] = grp(_P_MISC)
    gq_ref[...] = grp(_P_GQ).astype(BF16)
    gk_ref[...] = grp(_P_GK).astype(BF16)
    gv_ref[...] = grp(_P_GV).astype(BF16)
    gog_ref[...] = grp(_P_GOG).astype(BF16)


def _proj(h, g, w_all, seq):
    t, d = h.shape
    tm = min(TOK_TM, seq)
    widths = [(512, BF16), (128, BF16), (128, BF16), (256, BF16), (128, BF16), (128, BF16), (128, BF16),
              (128, F32), (256, BF16), (256, BF16), (512, BF16), (512, BF16)]
    return pl.pallas_call(
        functools.partial(_proj_kernel, seq),
        grid=(t // tm,),
        in_specs=[
            pl.BlockSpec((tm, d), lambda i: (i, 0)),
            pl.BlockSpec((1, d), lambda i: (0, 0)),
            pl.BlockSpec((d, _P_WIDTH), lambda i: (0, 0)),
        ],
        out_specs=[pl.BlockSpec((tm, w), lambda i: (i, 0)) for w, _ in widths],
        out_shape=[jax.ShapeDtypeStruct((t, w), dt) for w, dt in widths],
        compiler_params=_cparams(("parallel",)),
        name="proj",
    )(h, g, w_all)


def _compress_kernel(k_ref, v_ref, pek_ref, w1k_ref, w2k_ref, pev_ref, w1v_ref, w2v_ref, kc_ref, vc_ref):
    half = CMP_STRIDE * NSA_HEAD_DIM

    def one(x_ref, pe_ref, w1_ref, w2_ref):
        x = x_ref[...].astype(F32)
        n = x.shape[0]
        a = _dot((x + pe_ref[0:1, :]).astype(BF16), w1_ref[0:half, :])
        b = _dot((x + pe_ref[1:2, :]).astype(BF16), w1_ref[half:2 * half, :])
        hid = a + pltpu.roll(b, n - 1, 0)
        act = (hid * _sigmoid(hid)).astype(BF16)
        return _dot(act, w2_ref[...])

    kc_ref[...] = one(k_ref, pek_ref, w1k_ref, w2k_ref).astype(BF16)
    vc_ref[...] = one(v_ref, pev_ref, w1v_ref, w2v_ref).T.astype(BF16)


def _compress(k2, v2, pek, w1k, w2k, pev, w1v, w2v):
    b, n, w = k2.shape
    dk = w2k.shape[1]
    big = pl.BlockSpec((None, n, w), lambda i: (i, 0, 0))

    def full(a):
        return pl.BlockSpec(a.shape, lambda i: (0,) * a.ndim)

    return pl.pallas_call(
        _compress_kernel,
        grid=(b,),
        in_specs=[big, big, full(pek), full(w1k), full(w2k), full(pev), full(w1v), full(w2v)],
        out_specs=[pl.BlockSpec((None, n, dk), lambda i: (i, 0, 0)), pl.BlockSpec((None, dk, n), lambda i: (i, 0, 0))],
        out_shape=[jax.ShapeDtypeStruct((b, n, dk), BF16), jax.ShapeDtypeStruct((b, dk, n), BF16)],
        compiler_params=_cparams(("parallel",)),
        name="compress",
    )(k2, v2, pek, w1k, w2k, pev, w1v, w2v)


def _nsa_kernel(q_ref, kc_ref, vct_ref, ks_ref, vs_ref, kw_ref, vw_ref, misc_ref, ovt_ref, og_ref, o_ref,
                qaug_ref, m_ref, acc_ref, oc_ref, gate_ref):
    tq = NSA_TQ
    rows = NSA_HEADS * tq
    jd = pl.program_id(1)
    step0 = jd * NSA_TK
    n_c = kc_ref.shape[0]
    log_g = int(math.log2(CMP_STRIDE))
    row_i = lax.broadcasted_iota(jnp.int32, (rows, LANES), 0) & (tq - 1)
    lane = lax.broadcasted_iota(jnp.int32, (rows, LANES), 1)
    lane_q = lax.broadcasted_iota(jnp.int32, (1, rows), 1) & (tq - 1)
    tri_diag = jnp.where(lane <= row_i, 0.0, NEG)
    tri_far = jnp.where(lane > row_i, 0.0, NEG)
    q_grp = jnp.where(lane == ((row_i + 1) >> log_g), 1.0, 0.0).astype(BF16)
    n_minus_g = (lax.broadcasted_iota(jnp.int32, (n_c, LANES), 0)
                 - lax.broadcasted_iota(jnp.int32, (n_c, LANES), 1))

    def spread_gates(r):
        n_gate = N_BRANCH * NSA_HEADS
        spread = jnp.where(lax.broadcasted_iota(jnp.int32, (LANES, n_gate * LANES), 0)
                           == (lax.broadcasted_iota(jnp.int32, (LANES, n_gate * LANES), 1) >> int(math.log2(LANES))),
                           1.0, 0.0).astype(BF16)
        yield
        g_hi, g_lo = _split(_sigmoid(misc_ref[r * tq:(r + 1) * tq, :]))
        yield
        gate_ref[r * tq:(r + 1) * tq, :] = _dot(g_hi, spread) + _dot(g_lo, spread)
        yield

    def heads_to_rows(q):
        return jnp.concatenate([q[:, h * NSA_HEAD_DIM:(h + 1) * NSA_HEAD_DIM] for h in range(NSA_HEADS)], axis=0)

    def prepare(r, n_vis):
        n_blk = n_vis * CMP_STRIDE // SLC_BLOCK
        q0 = step0 + r * tq
        qh = heads_to_rows(q_ref[r * tq:(r + 1) * tq, :])
        vis = jnp.where(n_minus_g[0:n_vis] <= (q0 >> log_g) - 2, 0.0, NEG).astype(BF16)
        st = _dot_nt(jnp.concatenate([kc_ref[0:n_vis, :], vis], axis=1), jnp.concatenate([qh, q_grp], axis=1))
        yield
        et = jnp.exp2(st - jnp.max(st, axis=0, keepdims=True))
        inv = jnp.where(q0 + lane_q >= CMP_BLOCK - 1, 1.0 / jnp.sum(et, axis=0, keepdims=True), 0.0)
        p_t = et * inv
        p_bf = p_t.astype(BF16)
        p_sum = p_t[:, 0:tq]
        for h in range(1, NSA_HEADS):
            p_sum = p_sum + p_t[:, h * tq:(h + 1) * tq]
        p_hi, p_lo = _split(p_sum)
        yield
        oc_ref[r] = _dot(vct_ref[:, 0:n_vis], p_bf).T
        ovt = ovt_ref[0:n_blk, 0:n_vis]
        imp = _dot(ovt, p_hi) + _dot(ovt, p_lo)
        yield
        blk = lax.broadcasted_iota(jnp.int32, (n_blk, tq), 0)
        cur = (q0 + lax.broadcasted_iota(jnp.int32, (n_blk, tq), 1)) >> int(math.log2(SLC_BLOCK))
        forced = (blk == 0) | (blk == cur) | (blk == cur - 1)
        val = jnp.where(forced | (blk > cur), -1.0, imp)
        blk_f = blk.astype(F32)
        bias_t = jnp.where(forced, 0.0, NEG)
        for _ in range(SLC_TOPK - N_FORCED):
            top = jnp.max(val, axis=0, keepdims=True)
            first = jnp.min(jnp.where(val == top, blk_f, float(LANES)), axis=0, keepdims=True)
            pick = blk_f == first
            bias_t = jnp.where(pick, 0.0, bias_t)
            val = jnp.where(pick, -2.0, val)
        if n_blk < LANES:
            bias_t = jnp.concatenate([bias_t, jnp.full((LANES - n_blk, tq), NEG, F32)], axis=0)
        bias = bias_t.T.astype(BF16)
        qaug_ref[r] = jnp.concatenate([qh, jnp.concatenate([bias] * NSA_HEADS, axis=0)], axis=1)
        m_ref[r] = jnp.full((rows, LANES), NEG, F32)
        acc_ref[r] = jnp.zeros((rows, 2 * NSA_HEAD_DIM), F32)
        yield

    bucket_rows = min(CMP_BUCKET, n_c)
    steps_per_bucket = bucket_rows * CMP_STRIDE // NSA_TK
    for bucket in range(n_c // bucket_rows):
        @pl.when(jd // steps_per_bucket == bucket)
        def _(bucket=bucket):
            _trace_skewed([prepare(r, (bucket + 1) * bucket_rows) for r in range(NSA_SUB)]
                          + [spread_gates(r) for r in range(NSA_SUB)])

    def mask_last_chunk(sc):
        w = sc.shape[1]
        if w == LANES:
            return sc + tri_diag
        return jnp.concatenate([sc[:, :w - LANES], sc[:, w - LANES:] + tri_diag], axis=1)

    def with_ones(v):
        return jnp.concatenate([v, jnp.ones(v.shape, BF16)], axis=1)

    def flash(r, k0):
        sc = _dot_nt(qaug_ref[r], ks_ref[pl.ds(k0, NSA_TK), :])
        yield
        m_i = m_ref[r]
        m_n = jnp.maximum(m_i, jnp.max(sc, axis=1, keepdims=True))
        alpha = jnp.exp2(m_i - m_n)
        p = jnp.exp2(sc - jnp.concatenate([m_n] * (NSA_TK // LANES), axis=1)).astype(BF16)
        yield
        acc_ref[r] = (jnp.concatenate([alpha, alpha], axis=1) * acc_ref[r]
                      + _dot(p, with_ones(vs_ref[pl.ds(k0, NSA_TK), :])))
        m_ref[r] = m_n
        yield

    def full_tiles(j0, n):
        _trace_skewed(flash(r, pl.multiple_of((j0 + t) * NSA_TK, NSA_TK))
                      for t in range(n) for r in range(NSA_SUB))

    def tile_group(j, carry):
        full_tiles(j * NSA_UNROLL, NSA_UNROLL)
        return carry

    lax.fori_loop(0, jd // NSA_UNROLL, tile_group, 0)
    for rem in range(1, NSA_UNROLL):
        @pl.when(jd % NSA_UNROLL == rem)
        def _(rem=rem):
            full_tiles(jd - rem, rem)

    def finish(r, first_step):
        w = (r + 1) * tq
        d0 = pl.multiple_of(step0, NSA_TK)
        sc = mask_last_chunk(_dot_nt(qaug_ref[r], ks_ref[pl.ds(d0, w), :]))
        vt = vs_ref[pl.ds(d0, w), :]
        qh = qaug_ref[r][:, 0:NSA_HEAD_DIM]
        if first_step:
            sw = mask_last_chunk(_dot_nt(qh, kw_ref[0:w, :]))
            vw = vw_ref[0:w, :]
        else:
            w0 = pl.multiple_of(step0 + r * tq - WINDOW, tq)
            sw = _dot_nt(qh, kw_ref[pl.ds(w0, WIN_SLAB), :])
            sw = jnp.concatenate([sw[:, :LANES] + tri_far, sw[:, LANES:WIN_SLAB - LANES],
                                  sw[:, WIN_SLAB - LANES:] + tri_diag], axis=1)
            vw = vw_ref[pl.ds(w0, WIN_SLAB), :]
        yield
        m_i = m_ref[r]
        m_n = jnp.maximum(m_i, jnp.max(sc, axis=1, keepdims=True))
        alpha = jnp.exp2(m_i - m_n)
        p = jnp.exp2(sc - jnp.concatenate([m_n] * (w // LANES), axis=1)).astype(BF16)
        ew = jnp.exp2(sw - jnp.max(sw, axis=1, keepdims=True)).astype(BF16)
        yield
        acc = jnp.concatenate([alpha, alpha], axis=1) * acc_ref[r] + _dot(p, with_ones(vt))
        pv = _dot(ew, with_ones(vw))
        yield
        o_s = acc[:, 0:NSA_HEAD_DIM] * (1.0 / acc[:, NSA_HEAD_DIM:])
        o_w = pv[:, 0:NSA_HEAD_DIM] * (1.0 / pv[:, NSA_HEAD_DIM:])
        o_c = oc_ref[r]
        outs = []
        for h in range(NSA_HEADS):
            hr = slice(h * tq, (h + 1) * tq)
            gc, gs, gw = (gate_ref[r * tq:(r + 1) * tq,
                                   (N_BRANCH * h + k) * LANES:(N_BRANCH * h + k + 1) * LANES]
                          for k in range(N_BRANCH))
            o_h = gc * o_c[hr] + gs * o_s[hr] + gw * o_w[hr]
            outs.append(_rms(o_h, og_ref[:, h * NSA_HEAD_DIM:(h + 1) * NSA_HEAD_DIM]))
        o_ref[r * tq:(r + 1) * tq, :] = jnp.concatenate(outs, axis=1).astype(BF16)
        yield

    @pl.when(jd == 0)
    def _():
        _trace_skewed(finish(r, True) for r in range(NSA_SUB))

    @pl.when(jd > 0)
    def _():
        _trace_skewed(finish(r, False) for r in range(NSA_SUB))


def _nsa(nq, kc, vct, ks, vs, kw, vw, misc, ovt, og):
    b, s, wq = nq.shape
    ts = NSA_SUB * NSA_TQ
    rows = NSA_HEADS * NSA_TQ

    def per_batch(a):
        return pl.BlockSpec((None,) + a.shape[1:], lambda i, j: (i, 0, 0))

    def per_tile(a):
        return pl.BlockSpec((None, ts, a.shape[2]), lambda i, j: (i, j, 0))

    def full(a):
        return pl.BlockSpec(a.shape, lambda i, j: (0, 0))

    return pl.pallas_call(
        _nsa_kernel,
        grid=(b, s // ts),
        in_specs=[per_tile(nq), per_batch(kc), per_batch(vct), per_batch(ks), per_batch(vs), per_batch(kw),
                  per_batch(vw), per_tile(misc), full(ovt), full(og)],
        out_specs=pl.BlockSpec((None, ts, wq), lambda i, j: (i, j, 0)),
        out_shape=jax.ShapeDtypeStruct((b, s, wq), BF16),
        scratch_shapes=[pltpu.VMEM((NSA_SUB, rows, 2 * NSA_HEAD_DIM), BF16),
                        pltpu.VMEM((NSA_SUB, rows, LANES), F32),
                        pltpu.VMEM((NSA_SUB, rows, 2 * NSA_HEAD_DIM), F32),
                        pltpu.VMEM((NSA_SUB, rows, NSA_HEAD_DIM), F32),
                        pltpu.VMEM((ts, N_BRANCH * NSA_HEADS * LANES), F32)],
        compiler_params=_cparams(("parallel", "arbitrary")),
        name="nsa",
    )(nq, kc, vct, ks, vs, kw, vw, misc, ovt, og)


def _gla_kernel(q_ref, k_ref, v_ref, og_ref, misc_ref, up_ref, bias_ref, g_ref, o_ref, st_ref):
    c = GLA_CHUNK
    kw = GLA_HEADS * GLA_KEY_DIM
    tg = q_ref.shape[1]
    n_chunk = tg // c
    log_c = int(math.log2(c))

    @pl.when(pl.program_id(1) == 0)
    def _():
        st_ref[...] = jnp.zeros_like(st_ref)

    ri = lax.broadcasted_iota(jnp.int32, (tg, tg), 0)
    ci = lax.broadcasted_iota(jnp.int32, (tg, tg), 1)
    low = ((ri >> log_c) == (ci >> log_c)) & (ci <= ri)
    causal = jnp.where(low, 1.0, 0.0)
    tri = causal.astype(BF16)
    lane_head = lax.broadcasted_iota(jnp.int32, (1, kw), 1) >> int(math.log2(GLA_KEY_DIM))
    head_mask = [lane_head == h for h in range(GLA_HEADS)]
    u_hi, u_lo = _split(up_ref[...])

    def stack_heads(x):
        return jnp.concatenate([jnp.where(head_mask[h], x, 0.0) for h in range(GLA_HEADS)], axis=0)

    def sequence(bb):
        m_hi, m_lo = _split(misc_ref[bb])
        g_pre = _dot(m_hi, u_hi) + _dot(m_lo, u_hi) + _dot(m_hi, u_lo) + bias_ref[...]
        yield
        g_log = (jnp.minimum(g_pre, 0.0) - jnp.log(1.0 + jnp.exp(-jnp.abs(g_pre)))) * (1.0 / GLA_GATE_NORM)
        g_hi, g_lo = _split(g_log)
        yield
        bcum = _dot(tri, g_hi) + _dot(tri, g_lo)
        yield
        b_last = jnp.concatenate([jnp.broadcast_to(bcum[(cc + 1) * c - 1:(cc + 1) * c, :], (c, kw))
                                  for cc in range(n_chunk)], axis=0)
        qf = q_ref[bb].astype(F32)
        kf = k_ref[bb].astype(F32)
        v = v_ref[bb]
        q_dec = qf * jnp.exp(bcum) * (GLA_KEY_DIM ** -0.5)
        k_dec = (kf * jnp.exp(-bcum)).astype(BF16)
        k_end = kf * jnp.exp(b_last - bcum)
        q_heads = [jnp.where(head_mask[h], q_dec, 0.0).astype(BF16) for h in range(GLA_HEADS)]
        yield
        scores = [_dot_nt(q_heads[h], k_dec) for h in range(GLA_HEADS)]
        yield
        a = [(scores[h] * causal).astype(BF16) for h in range(GLA_HEADS)]
        yield
        intra = [_dot(a[h], v[:, h * GLA_VAL_DIM:(h + 1) * GLA_VAL_DIM]) for h in range(GLA_HEADS)]
        st = st_ref[bb]
        inter = []
        for cc in range(n_chunk):
            r = slice(cc * c, (cc + 1) * c)
            inter.append(_dot_nt(stack_heads(q_dec[r]).astype(BF16), st.astype(BF16)))
            km = stack_heads(k_end[r]).astype(BF16)
            vst = jnp.concatenate([v[r, h * GLA_VAL_DIM:(h + 1) * GLA_VAL_DIM] for h in range(GLA_HEADS)],
                                  axis=0)
            st = st * jnp.exp(b_last[cc * c:cc * c + 1, :]) + _dot_tn(vst, km)
        st_ref[bb] = st
        yield
        for h in range(GLA_HEADS):
            vs = slice(h * GLA_VAL_DIM, (h + 1) * GLA_VAL_DIM)
            inter_h = jnp.concatenate([inter[cc][h * c:(h + 1) * c] for cc in range(n_chunk)], axis=0)
            gate = og_ref[bb, :, vs].astype(F32)
            o_ref[bb, :, vs] = (_rms(intra[h] + inter_h, g_ref[:, vs]) * (gate * _sigmoid(gate))).astype(BF16)
        yield

    _trace_skewed(sequence(bb) for bb in range(q_ref.shape[0]))


def _gla(gq, gk, gv, gog, misc, up_pad, bias, g):
    b, s, _ = gq.shape
    tg = min(GLA_TG, s)
    nb = GLA_NB if b % GLA_NB == 0 else 1

    def per_tile(a):
        return pl.BlockSpec((nb, tg, a.shape[2]), lambda i, j: (i, j, 0))

    def full(a):
        return pl.BlockSpec(a.shape, lambda i, j: (0, 0))

    return pl.pallas_call(
        _gla_kernel,
        grid=(b // nb, s // tg),
        in_specs=[per_tile(gq), per_tile(gk), per_tile(gv), per_tile(gog), per_tile(misc),
                  full(up_pad), full(bias), full(g)],
        out_specs=per_tile(gv),
        out_shape=jax.ShapeDtypeStruct(gv.shape, BF16),
        scratch_shapes=[pltpu.VMEM((nb, GLA_VAL_DIM, GLA_HEADS * GLA_KEY_DIM), F32)],
        compiler_params=_cparams(("parallel", "arbitrary")),
        name="gla",
    )(gq, gk, gv, gog, misc, up_pad, bias, g)


def _overlap_matrix(n_rows):
    n = np.arange(n_rows)[:, None] * CMP_STRIDE
    s = np.arange(LANES)[None, :] * SLC_BLOCK
    return ((n < s + SLC_BLOCK) & (n + CMP_BLOCK > s)).astype(np.float32)


def _layer(h, p, seq, batch, ffn1_pre_g, ffn1_post_g, ffn1_w_gate, ffn1_w_up, ffn1_w_down, mix_pre_g, mix_post_g,
           w_in, cmp_k_pe, cmp_k_w1, cmp_k_w2, cmp_v_pe, cmp_v_w1, cmp_v_w2, nsa_out_g, gla_gate_up, gla_gate_bias,
           gla_out_g, w_out, ffn2_pre_g, ffn2_post_g, ffn2_w_gate, ffn2_w_up, ffn2_w_down, ple_proj, ple_gate,
           ple_post_g):
    d = h.shape[1]
    row = lambda a: a.reshape(1, -1).astype(F32)
    bf = lambda a: a.astype(BF16)

    h1 = _ffn(h, row(ffn1_pre_g), row(ffn1_post_g), bf(ffn1_w_gate), bf(ffn1_w_up), bf(ffn1_w_down))

    o_gate = 512 + 6 * 128
    o_gq = o_gate + 3 * NSA_HEADS
    o_gk = o_gq + GLA_HEADS * GLA_KEY_DIM
    o_gv = o_gk + GLA_HEADS * GLA_KEY_DIM
    o_gd = o_gv + GLA_HEADS * GLA_VAL_DIM
    o_og = o_gd + GLA_GATE_RANK
    pad = jnp.zeros((d, LANES - 3 * NSA_HEADS - GLA_GATE_RANK), w_in.dtype)
    w_all = bf(jnp.concatenate([w_in[:, :o_gate], w_in[:, o_gate:o_gq], w_in[:, o_gd:o_og], pad,
                                w_in[:, o_gq:o_gd], w_in[:, o_og:]], axis=1))
    nq, kc_in, vc_in, ks, vs, kw, vw, misc, gq, gk, gv, gog = _proj(h1, row(mix_pre_g), w_all, seq)

    n_grp = seq // CMP_STRIDE
    grp_w = CMP_STRIDE * NSA_HEAD_DIM
    kc, vc = _compress(
        kc_in.reshape(batch, n_grp, grp_w), vc_in.reshape(batch, n_grp, grp_w),
        cmp_k_pe.reshape(2, grp_w).astype(F32), bf(cmp_k_w1.reshape(2 * grp_w, -1)), bf(cmp_k_w2),
        cmp_v_pe.reshape(2, grp_w).astype(F32), bf(cmp_v_w1.reshape(2 * grp_w, -1)), bf(cmp_v_w2))

    b3 = lambda a: a.reshape(batch, seq, a.shape[-1])
    ovt = jnp.asarray(_overlap_matrix(n_grp).T, BF16)
    o_nsa = _nsa(b3(nq), kc, vc, b3(ks), b3(vs), b3(kw), b3(vw), b3(misc), ovt, row(nsa_out_g))

    up_pad = jnp.zeros((LANES, GLA_HEADS * GLA_KEY_DIM), F32)
    up_pad = up_pad.at[_MISC_GDOWN:_MISC_GDOWN + GLA_GATE_RANK].set(gla_gate_up.astype(F32))
    o_gla = _gla(b3(gq), b3(gk), b3(gv), b3(gog), b3(misc), up_pad, row(gla_gate_bias), row(gla_out_g))

    w_o = bf(w_out)
    half = NSA_HEADS * NSA_HEAD_DIM
    return _ffn(h1, row(ffn2_pre_g), row(ffn2_post_g), bf(ffn2_w_gate), bf(ffn2_w_up), bf(ffn2_w_down),
                mix=(o_nsa.reshape(-1, half), o_gla.reshape(-1, half), w_o[:half], w_o[half:], row(mix_post_g)),
                ple=(p, bf(ple_proj), bf(ple_gate), row(ple_post_g)))


def kernel(x, p, ffn1_pre_g, ffn1_post_g, ffn1_w_gate, ffn1_w_up, ffn1_w_down, mix_pre_g, mix_post_g, w_in, cmp_k_pe, cmp_k_w1, cmp_k_w2, cmp_v_pe, cmp_v_w1, cmp_v_w2, nsa_out_g, gla_gate_up, gla_gate_bias, gla_out_g, w_out, ffn2_pre_g, ffn2_post_g, ffn2_w_gate, ffn2_w_up, ffn2_w_down, ple_proj, ple_gate, ple_post_g):
    batch, seq, d = x.shape
    depth = p.shape[0]
    assert seq % NSA_TK == 0 and seq >= WIN_SLAB and seq // SLC_BLOCK <= LANES
    h = x.reshape(batch * seq, d)
    params = (ffn1_pre_g, ffn1_post_g, ffn1_w_gate, ffn1_w_up, ffn1_w_down, mix_pre_g, mix_post_g, w_in, cmp_k_pe,
              cmp_k_w1, cmp_k_w2, cmp_v_pe, cmp_v_w1, cmp_v_w2, nsa_out_g, gla_gate_up, gla_gate_bias, gla_out_g,
              w_out, ffn2_pre_g, ffn2_post_g, ffn2_w_gate, ffn2_w_up, ffn2_w_down, ple_proj, ple_gate, ple_post_g)
    for i in range(depth):
        h = _layer(h, p[i].reshape(batch * seq, -1), seq, batch, *[a[i] for a in params])
    return h.reshape(batch, seq, d)
```

```python
import functools
import math

import numpy as np
import jax
import jax.numpy as jnp
from jax import lax
from jax.experimental import pallas as pl
from jax.experimental.pallas import tpu as pltpu

F32 = jnp.float32
BF16 = jnp.bfloat16

EPS = 1e-6
NSA_HEADS = 4
NSA_HEAD_DIM = 128
NSA_KV_W = 128
CMP_BLOCK = 32
CMP_STRIDE = 16
SLC_BLOCK = 64
SLC_TOPK = 16
WINDOW = 512
FORCE_BONUS = 1e4
NEG = -1e30
GLA_HEADS = 4
GLA_KEY_DIM = 64
GLA_VAL_DIM = 128
GLA_GATE_RANK = 16
GLA_GATE_NORM = 16.0
GLA_CHUNK = 64

LOG2E = 1.4426950408889634
N_FORCED = 3
N_BRANCH = 3

LANES = 128
VMEM_LIMIT = 56 * 1024 * 1024

FFN_TM = 1024
FFN_TF = 256
TOK_TM = 1024
NSA_TQ = 128
NSA_SUB = 4
NSA_TK = 512
NSA_UNROLL = 4
NSA_TKL = 2048
CMP_BUCKET = 128
GLA_TG = 256
GLA_NB = 8
WIN_SLAB = WINDOW + NSA_TQ


def _dot(a, b):
    return jnp.dot(a, b, preferred_element_type=F32)


def _dot_nt(a, b):
    return lax.dot_general(a, b, (((1,), (1,)), ((), ())), preferred_element_type=F32)


def _dot_tn(a, b):
    return lax.dot_general(a, b, (((0,), (0,)), ((), ())), preferred_element_type=F32)


def _split(x):
    hi = x.astype(BF16)
    lo = (x - hi.astype(F32)).astype(BF16)
    return hi, lo


def _rms(x, g):
    return x * lax.rsqrt(jnp.mean(x * x, axis=-1, keepdims=True) + EPS) * g


def _sigmoid(x):
    return 1.0 / (1.0 + jnp.exp(-x))


def _cparams(sem):
    return pltpu.CompilerParams(dimension_semantics=sem, vmem_limit_bytes=VMEM_LIMIT)


def _trace_skewed(stage_generators):
    pending = list(stage_generators)
    running = []
    while running or pending:
        if pending:
            running.append(pending.pop(0))
        for gen in list(running):
            if next(gen, "done") == "done":
                running.remove(gen)


def _ffn_kernel(n_mix, n_ple, *refs):
    h_ref, pre_ref, post_ref, wg_ref, wu_ref, wd_ref = refs[:6]
    mix_refs = refs[6:6 + n_mix]
    ple_refs = refs[6 + n_mix:6 + n_mix + n_ple]
    o_ref = refs[-1]
    h = h_ref[...]
    if n_mix:
        on_ref, ogl_ref, wa_ref, wb_ref, mg_ref = mix_refs
        h = h + _rms(_dot(on_ref[...], wa_ref[...]) + _dot(ogl_ref[...], wb_ref[...]), mg_ref[...])
    u = _rms(h, pre_ref[...]).astype(BF16)
    acc = None
    for c in range(wg_ref.shape[1] // FFN_TF):
        sl = slice(c * FFN_TF, (c + 1) * FFN_TF)
        a = _dot(u, wg_ref[:, sl])
        b = _dot(u, wu_ref[:, sl])
        part = _dot((a * _sigmoid(a) * b).astype(BF16), wd_ref[sl, :])
        acc = part if acc is None else acc + part
    h = h + 0.5 * _rms(acc, post_ref[...])
    if n_ple:
        p_ref, wp_ref, wgate_ref, pg_ref = ple_refs
        e = _dot(p_ref[...].astype(BF16), wp_ref[...])
        h = h + _rms(e * _sigmoid(_dot(h.astype(BF16), wgate_ref[...])), pg_ref[...])
    o_ref[...] = h


def _resident(a):
    return pl.BlockSpec(a.shape, lambda *_: (0,) * a.ndim, pipeline_mode=pl.Buffered(1))


def _ffn(h, pre_g, post_g, wg, wu, wd, mix=(), ple=()):
    t, d = h.shape
    tm = min(FFN_TM, t)

    def tile(a):
        return pl.BlockSpec((tm, a.shape[1]), lambda i: (i, 0))

    args = [h, pre_g, post_g, wg, wu, wd, *mix, *ple]
    specs = [tile(h)] + [_resident(a) for a in (pre_g, post_g, wg, wu, wd)]
    if mix:
        specs += [tile(mix[0]), tile(mix[1])] + [_resident(a) for a in mix[2:]]
    if ple:
        specs += [tile(ple[0])] + [_resident(a) for a in ple[1:]]
    return pl.pallas_call(
        functools.partial(_ffn_kernel, len(mix), len(ple)),
        grid=(t // tm,),
        in_specs=specs,
        out_specs=tile(h),
        out_shape=jax.ShapeDtypeStruct((t, d), F32),
        compiler_params=_cparams(("parallel",)),
        name="ffn",
    )(*args)


_P_NQ = (0, 512)
_P_KC = (512, 640)
_P_VC = (640, 768)
_P_KS = (768, 896)
_P_VS = (896, 1024)
_P_KW = (1024, 1152)
_P_VW = (1152, 1280)
_P_MISC = (1280, 1408)
_P_GQ = (1408, 1664)
_P_GK = (1664, 1920)
_P_GV = (1920, 2432)
_P_GOG = (2432, 2944)
_P_WIDTH = 2944
_MISC_GDOWN = 3 * NSA_HEADS


def _proj_kernel(seq, h_ref, g_ref, w_ref, nq_ref, kc_ref, vc_ref, ks_ref, vs_ref, kw_ref, vw_ref,
                 misc_ref, gq_ref, gk_ref, gv_ref, gog_ref, kc_tmp, vc_tmp):
    tm = h_ref.shape[0]
    u = _rms(h_ref[...], g_ref[...]).astype(BF16)

    def grp(lohi):
        return _dot(u, w_ref[:, lohi[0]:lohi[1]])

    nq_ref[...] = (grp(_P_NQ) * (NSA_HEAD_DIM ** -0.5 * LOG2E)).astype(BF16)
    kv_c = grp((_P_KC[0], _P_VC[1]))
    kc_tmp[...] = kv_c[:, 0:LANES]
    vc_tmp[...] = kv_c[:, LANES:2 * LANES]
    for l in range(CMP_STRIDE):
        rows_l = pl.ds(l, tm // CMP_STRIDE, stride=CMP_STRIDE)
        kc_ref[:, l * LANES:(l + 1) * LANES] = kc_tmp[rows_l, :].astype(BF16)
        vc_ref[:, l * LANES:(l + 1) * LANES] = vc_tmp[rows_l, :].astype(BF16)
    base = (pl.program_id(0) % (seq // tm)) * tm
    row = lax.broadcasted_iota(jnp.int32, (tm, LANES), 0)
    lane = lax.broadcasted_iota(jnp.int32, (tm, LANES), 1)
    onehot = jnp.where(lane == ((base + row) >> int(math.log2(SLC_BLOCK))), 1.0, 0.0).astype(BF16)
    kv_s = grp((_P_KS[0], _P_VS[1])).astype(BF16)
    ks_ref[:, 0:LANES] = kv_s[:, 0:LANES]
    ks_ref[:, LANES:2 * LANES] = onehot
    vs_ref[...] = kv_s[:, LANES:2 * LANES]
    kv_w = grp((_P_KW[0], _P_VW[1])).astype(BF16)
    kw_ref[...] = kv_w[:, 0:LANES]
    vw_ref[...] = kv_w[:, LANES:2 * LANES]
    misc_ref[...] = grp(_P_MISC)
    gq_ref[...] = grp(_P_GQ).astype(BF16)
    gk_ref[...] = grp(_P_GK).astype(BF16)
    gv_ref[...] = grp(_P_GV).astype(BF16)
    gog_ref[...] = grp(_P_GOG).astype(BF16)


def _proj(h, g, w_all, seq):
    t, d = h.shape
    tm = min(TOK_TM, seq)
    grp_w = CMP_STRIDE * NSA_KV_W
    outs = [(tm, 512, BF16), (tm // CMP_STRIDE, grp_w, BF16), (tm // CMP_STRIDE, grp_w, BF16), (tm, 256, BF16),
            (tm, 128, BF16), (tm, 128, BF16), (tm, 128, BF16), (tm, 128, F32), (tm, 256, BF16), (tm, 256, BF16),
            (tm, 512, BF16), (tm, 512, BF16)]
    return pl.pallas_call(
        functools.partial(_proj_kernel, seq),
        grid=(t // tm,),
        in_specs=[
            pl.BlockSpec((tm, d), lambda i: (i, 0)),
            pl.BlockSpec((1, d), lambda i: (0, 0)),
            pl.BlockSpec((d, _P_WIDTH), lambda i: (0, 0)),
        ],
        out_specs=[pl.BlockSpec((r, w), lambda i: (i, 0)) for r, w, _ in outs],
        out_shape=[jax.ShapeDtypeStruct((t // tm * r, w), dt) for r, w, dt in outs],
        scratch_shapes=[pltpu.VMEM((tm, NSA_KV_W), F32), pltpu.VMEM((tm, NSA_KV_W), F32)],
        compiler_params=_cparams(("parallel",)),
        name="proj",
    )(h, g, w_all)


def _compress_kernel(k_ref, v_ref, pek_ref, w1k_ref, w2k_ref, pev_ref, w1v_ref, w2v_ref, kc_ref, vc_ref):
    half = CMP_STRIDE * NSA_HEAD_DIM

    def one(x_ref, pe_ref, w1_ref, w2_ref):
        x = x_ref[...].astype(F32)
        n = x.shape[0]
        a = _dot((x + pe_ref[0:1, :]).astype(BF16), w1_ref[0:half, :])
        b = _dot((x + pe_ref[1:2, :]).astype(BF16), w1_ref[half:2 * half, :])
        hid = a + pltpu.roll(b, n - 1, 0)
        act = (hid * _sigmoid(hid)).astype(BF16)
        return _dot(act, w2_ref[...])

    kc_ref[...] = one(k_ref, pek_ref, w1k_ref, w2k_ref).astype(BF16)
    vc_ref[...] = one(v_ref, pev_ref, w1v_ref, w2v_ref).T.astype(BF16)


def _compress(k2, v2, pek, w1k, w2k, pev, w1v, w2v):
    b, n, w = k2.shape
    dk = w2k.shape[1]
    big = pl.BlockSpec((None, n, w), lambda i: (i, 0, 0))

    def full(a):
        return pl.BlockSpec(a.shape, lambda i: (0,) * a.ndim)

    return pl.pallas_call(
        _compress_kernel,
        grid=(b,),
        in_specs=[big, big, full(pek), full(w1k), full(w2k), full(pev), full(w1v), full(w2v)],
        out_specs=[pl.BlockSpec((None, n, dk), lambda i: (i, 0, 0)), pl.BlockSpec((None, dk, n), lambda i: (i, 0, 0))],
        out_shape=[jax.ShapeDtypeStruct((b, n, dk), BF16), jax.ShapeDtypeStruct((b, dk, n), BF16)],
        compiler_params=_cparams(("parallel",)),
        name="compress",
    )(k2, v2, pek, w1k, w2k, pev, w1v, w2v)


def _nsa_kernel(q_ref, kc_ref, vct_ref, ks_ref, vs_ref, kw_ref, vw_ref, misc_ref, ovt_ref, og_ref, o_ref,
                qaug_ref, m_ref, acc_ref, oc_ref, gate_ref):
    tq = NSA_TQ
    rows = NSA_HEADS * tq
    jd = pl.program_id(1)
    step0 = jd * NSA_TK
    n_c = kc_ref.shape[0]
    log_g = int(math.log2(CMP_STRIDE))
    row_i = lax.broadcasted_iota(jnp.int32, (rows, LANES), 0) & (tq - 1)
    lane = lax.broadcasted_iota(jnp.int32, (rows, LANES), 1)
    lane_q = lax.broadcasted_iota(jnp.int32, (1, rows), 1) & (tq - 1)
    tri_diag = jnp.where(lane <= row_i, 0.0, NEG)
    tri_far = jnp.where(lane > row_i, 0.0, NEG)
    q_grp = jnp.where(lane == ((row_i + 1) >> log_g), 1.0, 0.0).astype(BF16)
    n_minus_g = (lax.broadcasted_iota(jnp.int32, (n_c, LANES), 0)
                 - lax.broadcasted_iota(jnp.int32, (n_c, LANES), 1))

    def spread_gates(r):
        n_gate = N_BRANCH * NSA_HEADS
        spread = jnp.where(lax.broadcasted_iota(jnp.int32, (LANES, n_gate * LANES), 0)
                           == (lax.broadcasted_iota(jnp.int32, (LANES, n_gate * LANES), 1) >> int(math.log2(LANES))),
                           1.0, 0.0).astype(BF16)
        yield
        g_hi, g_lo = _split(_sigmoid(misc_ref[r * tq:(r + 1) * tq, :]))
        yield
        gate_ref[r * tq:(r + 1) * tq, :] = _dot(g_hi, spread) + _dot(g_lo, spread)
        yield

    def heads_to_rows(q):
        return jnp.concatenate([q[:, h * NSA_HEAD_DIM:(h + 1) * NSA_HEAD_DIM] for h in range(NSA_HEADS)], axis=0)

    def prepare(r, n_vis):
        n_blk = n_vis * CMP_STRIDE // SLC_BLOCK
        q0 = step0 + r * tq
        qh = heads_to_rows(q_ref[r * tq:(r + 1) * tq, :])
        vis = jnp.where(n_minus_g[0:n_vis] <= (q0 >> log_g) - 2, 0.0, NEG).astype(BF16)
        st = _dot_nt(jnp.concatenate([kc_ref[0:n_vis, :], vis], axis=1), jnp.concatenate([qh, q_grp], axis=1))
        yield
        et = jnp.exp2(st - jnp.max(st, axis=0, keepdims=True))
        inv = jnp.where(q0 + lane_q >= CMP_BLOCK - 1, 1.0 / jnp.sum(et, axis=0, keepdims=True), 0.0)
        p_t = et * inv
        p_bf = p_t.astype(BF16)
        p_sum = p_t[:, 0:tq]
        for h in range(1, NSA_HEADS):
            p_sum = p_sum + p_t[:, h * tq:(h + 1) * tq]
        p_hi, p_lo = _split(p_sum)
        yield
        oc_ref[r] = _dot(vct_ref[:, 0:n_vis], p_bf).T
        ovt = ovt_ref[0:n_blk, 0:n_vis]
        imp = _dot(ovt, p_hi) + _dot(ovt, p_lo)
        yield
        blk = lax.broadcasted_iota(jnp.int32, (n_blk, tq), 0)
        cur = (q0 + lax.broadcasted_iota(jnp.int32, (n_blk, tq), 1)) >> int(math.log2(SLC_BLOCK))
        forced = (blk == 0) | (blk == cur) | (blk == cur - 1)
        val = jnp.where(forced | (blk > cur), -1.0, imp)
        blk_f = blk.astype(F32)
        bias_t = jnp.where(forced, 0.0, NEG)
        for _ in range(SLC_TOPK - N_FORCED):
            top = jnp.max(val, axis=0, keepdims=True)
            first = jnp.min(jnp.where(val == top, blk_f, float(LANES)), axis=0, keepdims=True)
            pick = blk_f == first
            bias_t = jnp.where(pick, 0.0, bias_t)
            val = jnp.where(pick, -2.0, val)
        if n_blk < LANES:
            bias_t = jnp.concatenate([bias_t, jnp.full((LANES - n_blk, tq), NEG, F32)], axis=0)
        bias = bias_t.T.astype(BF16)
        qaug_ref[r] = jnp.concatenate([qh, jnp.concatenate([bias] * NSA_HEADS, axis=0)], axis=1)
        m_ref[r] = jnp.full((rows, LANES), NEG, F32)
        acc_ref[r] = jnp.zeros((rows, 2 * NSA_HEAD_DIM), F32)
        yield

    bucket_rows = min(CMP_BUCKET, n_c)
    steps_per_bucket = bucket_rows * CMP_STRIDE // NSA_TK
    for bucket in range(n_c // bucket_rows):
        @pl.when(jd // steps_per_bucket == bucket)
        def _(bucket=bucket):
            _trace_skewed([prepare(r, (bucket + 1) * bucket_rows) for r in range(NSA_SUB)]
                          + [spread_gates(r) for r in range(NSA_SUB)])

    def mask_last_chunk(sc):
        w = sc.shape[1]
        if w == LANES:
            return sc + tri_diag
        return jnp.concatenate([sc[:, :w - LANES], sc[:, w - LANES:] + tri_diag], axis=1)

    def with_ones(v):
        return jnp.concatenate([v, jnp.ones(v.shape, BF16)], axis=1)

    def flash(r, k0, w):
        sc = _dot_nt(qaug_ref[r], ks_ref[pl.ds(k0, w), :])
        yield
        m_i = m_ref[r]
        m_n = jnp.maximum(m_i, jnp.max(sc, axis=1, keepdims=True))
        alpha = jnp.exp2(m_i - m_n)
        p = jnp.exp2(sc - jnp.concatenate([m_n] * (w // LANES), axis=1)).astype(BF16)
        yield
        acc_ref[r] = (jnp.concatenate([alpha, alpha], axis=1) * acc_ref[r]
                      + _dot(p, with_ones(vs_ref[pl.ds(k0, w), :])))
        m_ref[r] = m_n
        yield

    def full_tiles(j0, n):
        spans = [(t, min(NSA_TKL // NSA_TK, n - t)) for t in range(0, n, NSA_TKL // NSA_TK)]
        _trace_skewed(flash(r, pl.multiple_of((j0 + t) * NSA_TK, NSA_TK), cnt * NSA_TK)
                      for t, cnt in spans for r in range(NSA_SUB))

    def tile_group(j, carry):
        full_tiles(j * NSA_UNROLL, NSA_UNROLL)
        return carry

    lax.fori_loop(0, jd // NSA_UNROLL, tile_group, 0)
    for rem in range(1, NSA_UNROLL):
        @pl.when(jd % NSA_UNROLL == rem)
        def _(rem=rem):
            full_tiles(jd - rem, rem)

    def finish(r, first_step):
        w = (r + 1) * tq
        d0 = pl.multiple_of(step0, NSA_TK)
        sc = mask_last_chunk(_dot_nt(qaug_ref[r], ks_ref[pl.ds(d0, w), :]))
        vt = vs_ref[pl.ds(d0, w), :]
        qh = qaug_ref[r][:, 0:NSA_HEAD_DIM]
        if first_step:
            sw = mask_last_chunk(_dot_nt(qh, kw_ref[0:w, :]))
            vw = vw_ref[0:w, :]
        else:
            w0 = pl.multiple_of(step0 + r * tq - WINDOW, tq)
            sw = _dot_nt(qh, kw_ref[pl.ds(w0, WIN_SLAB), :])
            sw = jnp.concatenate([sw[:, :LANES] + tri_far, sw[:, LANES:WIN_SLAB - LANES],
                                  sw[:, WIN_SLAB - LANES:] + tri_diag], axis=1)
            vw = vw_ref[pl.ds(w0, WIN_SLAB), :]
        yield
        m_i = m_ref[r]
        m_n = jnp.maximum(m_i, jnp.max(sc, axis=1, keepdims=True))
        alpha = jnp.exp2(m_i - m_n)
        p = jnp.exp2(sc - jnp.concatenate([m_n] * (w // LANES), axis=1)).astype(BF16)
        ew = jnp.exp2(sw - jnp.max(sw, axis=1, keepdims=True)).astype(BF16)
        yield
        acc = jnp.concatenate([alpha, alpha], axis=1) * acc_ref[r] + _dot(p, with_ones(vt))
        pv = _dot(ew, with_ones(vw))
        yield
        o_s = acc[:, 0:NSA_HEAD_DIM] * (1.0 / acc[:, NSA_HEAD_DIM:])
        o_w = pv[:, 0:NSA_HEAD_DIM] * (1.0 / pv[:, NSA_HEAD_DIM:])
        o_c = oc_ref[r]
        outs = []
        for h in range(NSA_HEADS):
            hr = slice(h * tq, (h + 1) * tq)
            gc, gs, gw = (gate_ref[r * tq:(r + 1) * tq,
                                   (N_BRANCH * h + k) * LANES:(N_BRANCH * h + k + 1) * LANES]
                          for k in range(N_BRANCH))
            o_h = gc * o_c[hr] + gs * o_s[hr] + gw * o_w[hr]
            outs.append(_rms(o_h, og_ref[:, h * NSA_HEAD_DIM:(h + 1) * NSA_HEAD_DIM]))
        o_ref[r * tq:(r + 1) * tq, :] = jnp.concatenate(outs, axis=1).astype(BF16)
        yield

    @pl.when(jd == 0)
    def _():
        _trace_skewed(finish(r, True) for r in range(NSA_SUB))

    @pl.when(jd > 0)
    def _():
        _trace_skewed(finish(r, False) for r in range(NSA_SUB))


def _nsa(nq, kc, vct, ks, vs, kw, vw, misc, ovt, og):
    b, s, wq = nq.shape
    ts = NSA_SUB * NSA_TQ
    rows = NSA_HEADS * NSA_TQ

    def per_batch(a):
        return pl.BlockSpec((None,) + a.shape[1:], lambda i, j: (i, 0, 0))

    def per_tile(a):
        return pl.BlockSpec((None, ts, a.shape[2]), lambda i, j: (i, j, 0))

    def full(a):
        return pl.BlockSpec(a.shape, lambda i, j: (0, 0))

    return pl.pallas_call(
        _nsa_kernel,
        grid=(b, s // ts),
        in_specs=[per_tile(nq), per_batch(kc), per_batch(vct), per_batch(ks), per_batch(vs), per_batch(kw),
                  per_batch(vw), per_tile(misc), full(ovt), full(og)],
        out_specs=pl.BlockSpec((None, ts, wq), lambda i, j: (i, j, 0)),
        out_shape=jax.ShapeDtypeStruct((b, s, wq), BF16),
        scratch_shapes=[pltpu.VMEM((NSA_SUB, rows, 2 * NSA_HEAD_DIM), BF16),
                        pltpu.VMEM((NSA_SUB, rows, LANES), F32),
                        pltpu.VMEM((NSA_SUB, rows, 2 * NSA_HEAD_DIM), F32),
                        pltpu.VMEM((NSA_SUB, rows, NSA_HEAD_DIM), F32),
                        pltpu.VMEM((ts, N_BRANCH * NSA_HEADS * LANES), F32)],
        compiler_params=_cparams(("parallel", "arbitrary")),
        name="nsa",
    )(nq, kc, vct, ks, vs, kw, vw, misc, ovt, og)


def _gla_kernel(q_ref, k_ref, v_ref, og_ref, misc_ref, up_ref, bias_ref, g_ref, o_ref, st_ref):
    c = GLA_CHUNK
    kw = GLA_HEADS * GLA_KEY_DIM
    tg = q_ref.shape[1]
    n_chunk = tg // c
    log_c = int(math.log2(c))

    @pl.when(pl.program_id(1) == 0)
    def _():
        st_ref[...] = jnp.zeros_like(st_ref)

    ri = lax.broadcasted_iota(jnp.int32, (tg, tg), 0)
    ci = lax.broadcasted_iota(jnp.int32, (tg, tg), 1)
    low = ((ri >> log_c) == (ci >> log_c)) & (ci <= ri)
    causal = jnp.where(low, 1.0, 0.0)
    tri = causal.astype(BF16)
    lane_head = lax.broadcasted_iota(jnp.int32, (1, kw), 1) >> int(math.log2(GLA_KEY_DIM))
    head_mask = [lane_head == h for h in range(GLA_HEADS)]
    u_hi, u_lo = _split(up_ref[...])

    def stack_heads(x):
        return jnp.concatenate([jnp.where(head_mask[h], x, 0.0) for h in range(GLA_HEADS)], axis=0)

    def sequence(bb):
        m_hi, m_lo = _split(misc_ref[bb])
        g_pre = _dot(m_hi, u_hi) + _dot(m_lo, u_hi) + _dot(m_hi, u_lo) + bias_ref[...]
        yield
        g_log = (jnp.minimum(g_pre, 0.0) - jnp.log(1.0 + jnp.exp(-jnp.abs(g_pre)))) * (1.0 / GLA_GATE_NORM)
        g_hi, g_lo = _split(g_log)
        yield
        bcum = _dot(tri, g_hi) + _dot(tri, g_lo)
        yield
        b_last = jnp.concatenate([jnp.broadcast_to(bcum[(cc + 1) * c - 1:(cc + 1) * c, :], (c, kw))
                                  for cc in range(n_chunk)], axis=0)
        qf = q_ref[bb].astype(F32)
        kf = k_ref[bb].astype(F32)
        v = v_ref[bb]
        q_dec = qf * jnp.exp(bcum) * (GLA_KEY_DIM ** -0.5)
        k_dec = (kf * jnp.exp(-bcum)).astype(BF16)
        k_end = kf * jnp.exp(b_last - bcum)
        q_heads = [jnp.where(head_mask[h], q_dec, 0.0).astype(BF16) for h in range(GLA_HEADS)]
        yield
        scores = [_dot_nt(q_heads[h], k_dec) for h in range(GLA_HEADS)]
        yield
        a = [(scores[h] * causal).astype(BF16) for h in range(GLA_HEADS)]
        yield
        intra = [_dot(a[h], v[:, h * GLA_VAL_DIM:(h + 1) * GLA_VAL_DIM]) for h in range(GLA_HEADS)]
        st = st_ref[bb]
        inter = []
        for cc in range(n_chunk):
            r = slice(cc * c, (cc + 1) * c)
            inter.append(_dot_nt(stack_heads(q_dec[r]).astype(BF16), st.astype(BF16)))
            km = stack_heads(k_end[r]).astype(BF16)
            vst = jnp.concatenate([v[r, h * GLA_VAL_DIM:(h + 1) * GLA_VAL_DIM] for h in range(GLA_HEADS)],
                                  axis=0)
            st = st * jnp.exp(b_last[cc * c:cc * c + 1, :]) + _dot_tn(vst, km)
        st_ref[bb] = st
        yield
        for h in range(GLA_HEADS):
            vs = slice(h * GLA_VAL_DIM, (h + 1) * GLA_VAL_DIM)
            inter_h = jnp.concatenate([inter[cc][h * c:(h + 1) * c] for cc in range(n_chunk)], axis=0)
            gate = og_ref[bb, :, vs].astype(F32)
            o_ref[bb, :, vs] = (_rms(intra[h] + inter_h, g_ref[:, vs]) * (gate * _sigmoid(gate))).astype(BF16)
        yield

    _trace_skewed(sequence(bb) for bb in range(q_ref.shape[0]))


def _gla(gq, gk, gv, gog, misc, up_pad, bias, g):
    b, s, _ = gq.shape
    tg = min(GLA_TG, s)
    nb = GLA_NB if b % GLA_NB == 0 else 1

    def per_tile(a):
        return pl.BlockSpec((nb, tg, a.shape[2]), lambda i, j: (i, j, 0))

    def full(a):
        return pl.BlockSpec(a.shape, lambda i, j: (0, 0))

    return pl.pallas_call(
        _gla_kernel,
        grid=(b // nb, s // tg),
        in_specs=[per_tile(gq), per_tile(gk), per_tile(gv), per_tile(gog), per_tile(misc),
                  full(up_pad), full(bias), full(g)],
        out_specs=per_tile(gv),
        out_shape=jax.ShapeDtypeStruct(gv.shape, BF16),
        scratch_shapes=[pltpu.VMEM((nb, GLA_VAL_DIM, GLA_HEADS * GLA_KEY_DIM), F32)],
        compiler_params=_cparams(("parallel", "arbitrary")),
        name="gla",
    )(gq, gk, gv, gog, misc, up_pad, bias, g)


def _overlap_matrix(n_rows):
    n = np.arange(n_rows)[:, None] * CMP_STRIDE
    s = np.arange(LANES)[None, :] * SLC_BLOCK
    return ((n < s + SLC_BLOCK) & (n + CMP_BLOCK > s)).astype(np.float32)


def _layer(h, p, seq, batch, ffn1_pre_g, ffn1_post_g, ffn1_w_gate, ffn1_w_up, ffn1_w_down, mix_pre_g, mix_post_g,
           w_in, cmp_k_pe, cmp_k_w1, cmp_k_w2, cmp_v_pe, cmp_v_w1, cmp_v_w2, nsa_out_g, gla_gate_up, gla_gate_bias,
           gla_out_g, w_out, ffn2_pre_g, ffn2_post_g, ffn2_w_gate, ffn2_w_up, ffn2_w_down, ple_proj, ple_gate,
           ple_post_g):
    d = h.shape[1]
    row = lambda a: a.reshape(1, -1).astype(F32)
    bf = lambda a: a.astype(BF16)

    h1 = _ffn(h, row(ffn1_pre_g), row(ffn1_post_g), bf(ffn1_w_gate), bf(ffn1_w_up), bf(ffn1_w_down))

    o_gate = 512 + 6 * 128
    o_gq = o_gate + 3 * NSA_HEADS
    o_gk = o_gq + GLA_HEADS * GLA_KEY_DIM
    o_gv = o_gk + GLA_HEADS * GLA_KEY_DIM
    o_gd = o_gv + GLA_HEADS * GLA_VAL_DIM
    o_og = o_gd + GLA_GATE_RANK
    pad = jnp.zeros((d, LANES - 3 * NSA_HEADS - GLA_GATE_RANK), w_in.dtype)
    w_all = bf(jnp.concatenate([w_in[:, :o_gate], w_in[:, o_gate:o_gq], w_in[:, o_gd:o_og], pad,
                                w_in[:, o_gq:o_gd], w_in[:, o_og:]], axis=1))
    nq, kc_in, vc_in, ks, vs, kw, vw, misc, gq, gk, gv, gog = _proj(h1, row(mix_pre_g), w_all, seq)

    n_grp = seq // CMP_STRIDE
    grp_w = CMP_STRIDE * NSA_HEAD_DIM
    kc, vc = _compress(
        kc_in.reshape(batch, n_grp, grp_w), vc_in.reshape(batch, n_grp, grp_w),
        cmp_k_pe.reshape(2, grp_w).astype(F32), bf(cmp_k_w1.reshape(2 * grp_w, -1)), bf(cmp_k_w2),
        cmp_v_pe.reshape(2, grp_w).astype(F32), bf(cmp_v_w1.reshape(2 * grp_w, -1)), bf(cmp_v_w2))

    b3 = lambda a: a.reshape(batch, seq, a.shape[-1])
    ovt = jnp.asarray(_overlap_matrix(n_grp).T, BF16)
    o_nsa = _nsa(b3(nq), kc, vc, b3(ks), b3(vs), b3(kw), b3(vw), b3(misc), ovt, row(nsa_out_g))

    up_pad = jnp.zeros((LANES, GLA_HEADS * GLA_KEY_DIM), F32)
    up_pad = up_pad.at[_MISC_GDOWN:_MISC_GDOWN + GLA_GATE_RANK].set(gla_gate_up.astype(F32))
    o_gla = _gla(b3(gq), b3(gk), b3(gv), b3(gog), b3(misc), up_pad, row(gla_gate_bias), row(gla_out_g))

    w_o = bf(w_out)
    half = NSA_HEADS * NSA_HEAD_DIM
    return _ffn(h1, row(ffn2_pre_g), row(ffn2_post_g), bf(ffn2_w_gate), bf(ffn2_w_up), bf(ffn2_w_down),
                mix=(o_nsa.reshape(-1, half), o_gla.reshape(-1, half), w_o[:half], w_o[half:], row(mix_post_g)),
                ple=(p, bf(ple_proj), bf(ple_gate), row(ple_post_g)))


def kernel(x, p, ffn1_pre_g, ffn1_post_g, ffn1_w_gate, ffn1_w_up, ffn1_w_down, mix_pre_g, mix_post_g, w_in, cmp_k_pe, cmp_k_w1, cmp_k_w2, cmp_v_pe, cmp_v_w1, cmp_v_w2, nsa_out_g, gla_gate_up, gla_gate_bias, gla_out_g, w_out, ffn2_pre_g, ffn2_post_g, ffn2_w_gate, ffn2_w_up, ffn2_w_down, ple_proj, ple_gate, ple_post_g):
    batch, seq, d = x.shape
    depth = p.shape[0]
    assert seq % NSA_TK == 0 and seq >= WIN_SLAB and seq // SLC_BLOCK <= LANES
    h = x.reshape(batch * seq, d)
    params = (ffn1_pre_g, ffn1_post_g, ffn1_w_gate, ffn1_w_up, ffn1_w_down, mix_pre_g, mix_post_g, w_in, cmp_k_pe,
              cmp_k_w1, cmp_k_w2, cmp_v_pe, cmp_v_w1, cmp_v_w2, nsa_out_g, gla_gate_up, gla_gate_bias, gla_out_g,
              w_out, ffn2_pre_g, ffn2_post_g, ffn2_w_gate, ffn2_w_up, ffn2_w_down, ple_proj, ple_gate, ple_post_g)
    for i in range(depth):
        h = _layer(h, p[i].reshape(batch * seq, -1), seq, batch, *[a[i] for a in params])
    return h.reshape(batch, seq, d)
```

```python
import functools
import math

import numpy as np
import jax
import jax.numpy as jnp
from jax import lax
from jax.experimental import pallas as pl
from jax.experimental.pallas import tpu as pltpu

F32 = jnp.float32
BF16 = jnp.bfloat16

EPS = 1e-6
NSA_HEADS = 4
NSA_HEAD_DIM = 128
NSA_KV_W = 128
CMP_BLOCK = 32
CMP_STRIDE = 16
SLC_BLOCK = 64
SLC_TOPK = 16
WINDOW = 512
FORCE_BONUS = 1e4
NEG = -1e30
GLA_HEADS = 4
GLA_KEY_DIM = 64
GLA_VAL_DIM = 128
GLA_GATE_RANK = 16
GLA_GATE_NORM = 16.0
GLA_CHUNK = 64

LOG2E = 1.4426950408889634
N_FORCED = 3
N_BRANCH = 3

LANES = 128
VMEM_LIMIT = 56 * 1024 * 1024

FFN_TM = 1024
FFN_TF = 256
TOK_TM = 1024
NSA_TQ = 128
NSA_SUB = 4
NSA_TK = 512
NSA_UNROLL = 4
NSA_TKL = 2048
CMP_BUCKET = 256
GLA_TG = 256
GLA_NB = 8
WIN_SLAB = WINDOW + NSA_TQ


def _dot(a, b):
    return jnp.dot(a, b, preferred_element_type=F32)


def _dot_nt(a, b):
    return lax.dot_general(a, b, (((1,), (1,)), ((), ())), preferred_element_type=F32)


def _dot_tn(a, b):
    return lax.dot_general(a, b, (((0,), (0,)), ((), ())), preferred_element_type=F32)


def _split(x):
    hi = x.astype(BF16)
    lo = (x - hi.astype(F32)).astype(BF16)
    return hi, lo


def _rms(x, g):
    return x * lax.rsqrt(jnp.mean(x * x, axis=-1, keepdims=True) + EPS) * g


def _sigmoid(x):
    return 1.0 / (1.0 + jnp.exp(-x))


def _cparams(sem):
    return pltpu.CompilerParams(dimension_semantics=sem, vmem_limit_bytes=VMEM_LIMIT)


def _trace_skewed(stage_generators):
    pending = list(stage_generators)
    running = []
    while running or pending:
        if pending:
            running.append(pending.pop(0))
        for gen in list(running):
            if next(gen, "done") == "done":
                running.remove(gen)


def _ffn_kernel(n_mix, n_ple, *refs):
    h_ref, pre_ref, post_ref, wg_ref, wu_ref, wd_ref = refs[:6]
    mix_refs = refs[6:6 + n_mix]
    ple_refs = refs[6 + n_mix:6 + n_mix + n_ple]
    o_ref = refs[-1]
    h = h_ref[...]
    if n_mix:
        on_ref, ogl_ref, wa_ref, wb_ref, mg_ref = mix_refs
        h = h + _rms(_dot(on_ref[...], wa_ref[...]) + _dot(ogl_ref[...], wb_ref[...]), mg_ref[...])
    u = _rms(h, pre_ref[...]).astype(BF16)
    acc = None
    for c in range(wg_ref.shape[1] // FFN_TF):
        sl = slice(c * FFN_TF, (c + 1) * FFN_TF)
        a = _dot(u, wg_ref[:, sl])
        b = _dot(u, wu_ref[:, sl])
        part = _dot((a * _sigmoid(a) * b).astype(BF16), wd_ref[sl, :])
        acc = part if acc is None else acc + part
    h = h + 0.5 * _rms(acc, post_ref[...])
    if n_ple:
        p_ref, wp_ref, wgate_ref, pg_ref = ple_refs
        e = _dot(p_ref[...].astype(BF16), wp_ref[...])
        h = h + _rms(e * _sigmoid(_dot(h.astype(BF16), wgate_ref[...])), pg_ref[...])
    o_ref[...] = h


def _resident(a):
    return pl.BlockSpec(a.shape, lambda *_: (0,) * a.ndim, pipeline_mode=pl.Buffered(1))


def _ffn(h, pre_g, post_g, wg, wu, wd, mix=(), ple=()):
    t, d = h.shape
    tm = min(FFN_TM, t)

    def tile(a):
        return pl.BlockSpec((tm, a.shape[1]), lambda i: (i, 0))

    args = [h, pre_g, post_g, wg, wu, wd, *mix, *ple]
    specs = [tile(h)] + [_resident(a) for a in (pre_g, post_g, wg, wu, wd)]
    if mix:
        specs += [tile(mix[0]), tile(mix[1])] + [_resident(a) for a in mix[2:]]
    if ple:
        specs += [tile(ple[0])] + [_resident(a) for a in ple[1:]]
    return pl.pallas_call(
        functools.partial(_ffn_kernel, len(mix), len(ple)),
        grid=(t // tm,),
        in_specs=specs,
        out_specs=tile(h),
        out_shape=jax.ShapeDtypeStruct((t, d), F32),
        compiler_params=_cparams(("parallel",)),
        name="ffn",
    )(*args)


_P_NQ = (0, 512)
_P_KC = (512, 640)
_P_VC = (640, 768)
_P_KS = (768, 896)
_P_VS = (896, 1024)
_P_KW = (1024, 1152)
_P_VW = (1152, 1280)
_P_MISC = (1280, 1408)
_P_GQ = (1408, 1664)
_P_GK = (1664, 1920)
_P_GV = (1920, 2432)
_P_GOG = (2432, 2944)
_P_WIDTH = 2944
_MISC_GDOWN = 3 * NSA_HEADS


def _proj_kernel(seq, h_ref, g_ref, w_ref, nq_ref, kc_ref, vc_ref, ks_ref, vs_ref, kw_ref, vw_ref,
                 misc_ref, gq_ref, gk_ref, gv_ref, gog_ref, kc_tmp, vc_tmp):
    tm = h_ref.shape[0]
    u = _rms(h_ref[...], g_ref[...]).astype(BF16)

    def grp(lohi):
        return _dot(u, w_ref[:, lohi[0]:lohi[1]])

    nq_ref[...] = (grp(_P_NQ) * (NSA_HEAD_DIM ** -0.5 * LOG2E)).astype(BF16)
    kv_c = grp((_P_KC[0], _P_VC[1]))
    kc_tmp[...] = kv_c[:, 0:LANES]
    vc_tmp[...] = kv_c[:, LANES:2 * LANES]
    for l in range(CMP_STRIDE):
        rows_l = pl.ds(l, tm // CMP_STRIDE, stride=CMP_STRIDE)
        kc_ref[:, l * LANES:(l + 1) * LANES] = kc_tmp[rows_l, :].astype(BF16)
        vc_ref[:, l * LANES:(l + 1) * LANES] = vc_tmp[rows_l, :].astype(BF16)
    base = (pl.program_id(0) % (seq // tm)) * tm
    row = lax.broadcasted_iota(jnp.int32, (tm, LANES), 0)
    lane = lax.broadcasted_iota(jnp.int32, (tm, LANES), 1)
    onehot = jnp.where(lane == ((base + row) >> int(math.log2(SLC_BLOCK))), 1.0, 0.0).astype(BF16)
    kv_s = grp((_P_KS[0], _P_VS[1])).astype(BF16)
    ks_ref[:, 0:LANES] = kv_s[:, 0:LANES]
    ks_ref[:, LANES:2 * LANES] = onehot
    vs_ref[...] = kv_s[:, LANES:2 * LANES]
    kv_w = grp((_P_KW[0], _P_VW[1])).astype(BF16)
    kw_ref[...] = kv_w[:, 0:LANES]
    vw_ref[...] = kv_w[:, LANES:2 * LANES]
    misc_ref[...] = grp(_P_MISC)
    gq_ref[...] = grp(_P_GQ).astype(BF16)
    gk_ref[...] = grp(_P_GK).astype(BF16)
    gv_ref[...] = grp(_P_GV).astype(BF16)
    gog_ref[...] = grp(_P_GOG).astype(BF16)


def _proj(h, g, w_all, seq):
    t, d = h.shape
    tm = min(TOK_TM, seq)
    grp_w = CMP_STRIDE * NSA_KV_W
    outs = [(tm, 512, BF16), (tm // CMP_STRIDE, grp_w, BF16), (tm // CMP_STRIDE, grp_w, BF16), (tm, 256, BF16),
            (tm, 128, BF16), (tm, 128, BF16), (tm, 128, BF16), (tm, 128, F32), (tm, 256, BF16), (tm, 256, BF16),
            (tm, 512, BF16), (tm, 512, BF16)]
    return pl.pallas_call(
        functools.partial(_proj_kernel, seq),
        grid=(t // tm,),
        in_specs=[
            pl.BlockSpec((tm, d), lambda i: (i, 0)),
            pl.BlockSpec((1, d), lambda i: (0, 0)),
            pl.BlockSpec((d, _P_WIDTH), lambda i: (0, 0)),
        ],
        out_specs=[pl.BlockSpec((r, w), lambda i: (i, 0)) for r, w, _ in outs],
        out_shape=[jax.ShapeDtypeStruct((t // tm * r, w), dt) for r, w, dt in outs],
        scratch_shapes=[pltpu.VMEM((tm, NSA_KV_W), F32), pltpu.VMEM((tm, NSA_KV_W), F32)],
        compiler_params=_cparams(("parallel",)),
        name="proj",
    )(h, g, w_all)


def _compress_kernel(k_ref, v_ref, pek_ref, w1k_ref, w2k_ref, pev_ref, w1v_ref, w2v_ref, kc_ref, vc_ref):
    half = CMP_STRIDE * NSA_HEAD_DIM

    def one(x_ref, pe_ref, w1_ref, w2_ref):
        x = x_ref[...].astype(F32)
        n = x.shape[0]
        a = _dot((x + pe_ref[0:1, :]).astype(BF16), w1_ref[0:half, :])
        b = _dot((x + pe_ref[1:2, :]).astype(BF16), w1_ref[half:2 * half, :])
        hid = a + pltpu.roll(b, n - 1, 0)
        act = (hid * _sigmoid(hid)).astype(BF16)
        return _dot(act, w2_ref[...])

    kc_ref[...] = one(k_ref, pek_ref, w1k_ref, w2k_ref).astype(BF16)
    vc_ref[...] = one(v_ref, pev_ref, w1v_ref, w2v_ref).T.astype(BF16)


def _compress(k2, v2, pek, w1k, w2k, pev, w1v, w2v):
    b, n, w = k2.shape
    dk = w2k.shape[1]
    big = pl.BlockSpec((None, n, w), lambda i: (i, 0, 0))

    def full(a):
        return pl.BlockSpec(a.shape, lambda i: (0,) * a.ndim)

    return pl.pallas_call(
        _compress_kernel,
        grid=(b,),
        in_specs=[big, big, full(pek), full(w1k), full(w2k), full(pev), full(w1v), full(w2v)],
        out_specs=[pl.BlockSpec((None, n, dk), lambda i: (i, 0, 0)), pl.BlockSpec((None, dk, n), lambda i: (i, 0, 0))],
        out_shape=[jax.ShapeDtypeStruct((b, n, dk), BF16), jax.ShapeDtypeStruct((b, dk, n), BF16)],
        compiler_params=_cparams(("parallel",)),
        name="compress",
    )(k2, v2, pek, w1k, w2k, pev, w1v, w2v)


def _nsa_kernel(q_ref, kc_ref, vct_ref, ks_ref, vs_ref, kw_ref, vw_ref, misc_ref, ovt_ref, og_ref, o_ref,
                qaug_ref, m_ref, acc_ref, oc_ref, gate_ref):
    tq = NSA_TQ
    rows = NSA_HEADS * tq
    jd = pl.program_id(1)
    step0 = jd * NSA_TK
    n_c = kc_ref.shape[0]
    log_g = int(math.log2(CMP_STRIDE))
    row_i = lax.broadcasted_iota(jnp.int32, (rows, LANES), 0) & (tq - 1)
    lane = lax.broadcasted_iota(jnp.int32, (rows, LANES), 1)
    lane_q = lax.broadcasted_iota(jnp.int32, (1, rows), 1) & (tq - 1)
    tri_diag = jnp.where(lane <= row_i, 0.0, NEG)
    tri_far = jnp.where(lane > row_i, 0.0, NEG)
    q_grp = jnp.where(lane == ((row_i + 1) >> log_g), 1.0, 0.0).astype(BF16)
    n_minus_g = (lax.broadcasted_iota(jnp.int32, (n_c, LANES), 0)
                 - lax.broadcasted_iota(jnp.int32, (n_c, LANES), 1))

    def spread_gates(r):
        n_gate = N_BRANCH * NSA_HEADS
        spread = jnp.where(lax.broadcasted_iota(jnp.int32, (LANES, n_gate * LANES), 0)
                           == (lax.broadcasted_iota(jnp.int32, (LANES, n_gate * LANES), 1) >> int(math.log2(LANES))),
                           1.0, 0.0).astype(BF16)
        yield
        g_hi, g_lo = _split(_sigmoid(misc_ref[r * tq:(r + 1) * tq, :]))
        yield
        gate_ref[r * tq:(r + 1) * tq, :] = _dot(g_hi, spread) + _dot(g_lo, spread)
        yield

    def heads_to_rows(q):
        return jnp.concatenate([q[:, h * NSA_HEAD_DIM:(h + 1) * NSA_HEAD_DIM] for h in range(NSA_HEADS)], axis=0)

    def prepare(r, n_vis):
        n_blk = n_vis * CMP_STRIDE // SLC_BLOCK
        q0 = step0 + r * tq
        qh = heads_to_rows(q_ref[r * tq:(r + 1) * tq, :])
        vis = jnp.where(n_minus_g[0:n_vis] <= (q0 >> log_g) - 2, 0.0, NEG).astype(BF16)
        st = _dot_nt(jnp.concatenate([kc_ref[0:n_vis, :], vis], axis=1), jnp.concatenate([qh, q_grp], axis=1))
        yield
        et = jnp.exp2(st - jnp.max(st, axis=0, keepdims=True))
        inv = jnp.where(q0 + lane_q >= CMP_BLOCK - 1, 1.0 / jnp.sum(et, axis=0, keepdims=True), 0.0)
        p_t = et * inv
        p_bf = p_t.astype(BF16)
        p_sum = p_t[:, 0:tq]
        for h in range(1, NSA_HEADS):
            p_sum = p_sum + p_t[:, h * tq:(h + 1) * tq]
        p_hi, p_lo = _split(p_sum)
        yield
        oc_ref[r] = _dot(vct_ref[:, 0:n_vis], p_bf).T
        ovt = ovt_ref[0:n_blk, 0:n_vis]
        imp = _dot(ovt, p_hi) + _dot(ovt, p_lo)
        yield
        blk = lax.broadcasted_iota(jnp.int32, (n_blk, tq), 0)
        cur = (q0 + lax.broadcasted_iota(jnp.int32, (n_blk, tq), 1)) >> int(math.log2(SLC_BLOCK))
        forced = (blk == 0) | (blk == cur) | (blk == cur - 1)
        val = jnp.where(forced | (blk > cur), -1.0, imp)
        blk_f = blk.astype(F32)
        for _ in range(SLC_TOPK - N_FORCED):
            top = jnp.max(val, axis=0, keepdims=True)
            first = jnp.min(jnp.where(val == top, blk_f, float(LANES)), axis=0, keepdims=True)
            val = jnp.where(blk_f == first, -2.0, val)
        bias_t = jnp.where(forced | (val == -2.0), 0.0, NEG)
        if n_blk < LANES:
            bias_t = jnp.concatenate([bias_t, jnp.full((LANES - n_blk, tq), NEG, F32)], axis=0)
        bias = bias_t.T.astype(BF16)
        qaug_ref[r] = jnp.concatenate([qh, jnp.concatenate([bias] * NSA_HEADS, axis=0)], axis=1)
        m_ref[r] = jnp.full((rows, LANES), NEG, F32)
        acc_ref[r] = jnp.zeros((rows, 2 * NSA_HEAD_DIM), F32)
        yield

    bucket_rows = min(CMP_BUCKET, n_c)
    steps_per_bucket = bucket_rows * CMP_STRIDE // NSA_TK
    for bucket in range(n_c // bucket_rows):
        @pl.when(jd // steps_per_bucket == bucket)
        def _(bucket=bucket):
            _trace_skewed([prepare(r, (bucket + 1) * bucket_rows) for r in range(NSA_SUB)]
                          + [spread_gates(r) for r in range(NSA_SUB)])

    def mask_last_chunk(sc):
        w = sc.shape[1]
        if w == LANES:
            return sc + tri_diag
        return jnp.concatenate([sc[:, :w - LANES], sc[:, w - LANES:] + tri_diag], axis=1)

    def with_ones(v):
        return jnp.concatenate([v, jnp.ones(v.shape, BF16)], axis=1)

    def flash(r, k0, w):
        sc = _dot_nt(qaug_ref[r], ks_ref[pl.ds(k0, w), :])
        yield
        m_i = m_ref[r]
        m_n = jnp.maximum(m_i, jnp.max(sc, axis=1, keepdims=True))
        alpha = jnp.exp2(m_i - m_n)
        p = jnp.exp2(sc - jnp.concatenate([m_n] * (w // LANES), axis=1)).astype(BF16)
        yield
        acc_ref[r] = (jnp.concatenate([alpha, alpha], axis=1) * acc_ref[r]
                      + _dot(p, with_ones(vs_ref[pl.ds(k0, w), :])))
        m_ref[r] = m_n
        yield

    def full_tiles(j0, n):
        spans = [(t, min(NSA_TKL // NSA_TK, n - t)) for t in range(0, n, NSA_TKL // NSA_TK)]
        _trace_skewed(flash(r, pl.multiple_of((j0 + t) * NSA_TK, NSA_TK), cnt * NSA_TK)
                      for t, cnt in spans for r in range(NSA_SUB))

    def tile_group(j, carry):
        full_tiles(j * NSA_UNROLL, NSA_UNROLL)
        return carry

    lax.fori_loop(0, jd // NSA_UNROLL, tile_group, 0)
    for rem in range(1, NSA_UNROLL):
        @pl.when(jd % NSA_UNROLL == rem)
        def _(rem=rem):
            full_tiles(jd - rem, rem)

    def finish(r, first_step):
        w = (r + 1) * tq
        d0 = pl.multiple_of(step0, NSA_TK)
        sc = mask_last_chunk(_dot_nt(qaug_ref[r], ks_ref[pl.ds(d0, w), :]))
        vt = vs_ref[pl.ds(d0, w), :]
        qh = qaug_ref[r][:, 0:NSA_HEAD_DIM]
        if first_step:
            sw = mask_last_chunk(_dot_nt(qh, kw_ref[0:w, :]))
            vw = vw_ref[0:w, :]
        else:
            w0 = pl.multiple_of(step0 + r * tq - WINDOW, tq)
            sw = _dot_nt(qh, kw_ref[pl.ds(w0, WIN_SLAB), :])
            sw = jnp.concatenate([sw[:, :LANES] + tri_far, sw[:, LANES:WIN_SLAB - LANES],
                                  sw[:, WIN_SLAB - LANES:] + tri_diag], axis=1)
            vw = vw_ref[pl.ds(w0, WIN_SLAB), :]
        yield
        m_i = m_ref[r]
        m_n = jnp.maximum(m_i, jnp.max(sc, axis=1, keepdims=True))
        alpha = jnp.exp2(m_i - m_n)
        p = jnp.exp2(sc - jnp.concatenate([m_n] * (w // LANES), axis=1)).astype(BF16)
        ew = jnp.exp2(sw - jnp.max(sw, axis=1, keepdims=True)).astype(BF16)
        yield
        acc = jnp.concatenate([alpha, alpha], axis=1) * acc_ref[r] + _dot(p, with_ones(vt))
        pv = _dot(ew, with_ones(vw))
        yield
        o_s = acc[:, 0:NSA_HEAD_DIM] * (1.0 / acc[:, NSA_HEAD_DIM:])
        o_w = pv[:, 0:NSA_HEAD_DIM] * (1.0 / pv[:, NSA_HEAD_DIM:])
        o_c = oc_ref[r]
        outs = []
        for h in range(NSA_HEADS):
            hr = slice(h * tq, (h + 1) * tq)
            gc, gs, gw = (gate_ref[r * tq:(r + 1) * tq,
                                   (N_BRANCH * h + k) * LANES:(N_BRANCH * h + k + 1) * LANES]
                          for k in range(N_BRANCH))
            o_h = gc * o_c[hr] + gs * o_s[hr] + gw * o_w[hr]
            outs.append(_rms(o_h, og_ref[:, h * NSA_HEAD_DIM:(h + 1) * NSA_HEAD_DIM]))
        o_ref[r * tq:(r + 1) * tq, :] = jnp.concatenate(outs, axis=1).astype(BF16)
        yield

    @pl.when(jd == 0)
    def _():
        _trace_skewed(finish(r, True) for r in range(NSA_SUB))

    @pl.when(jd > 0)
    def _():
        _trace_skewed(finish(r, False) for r in range(NSA_SUB))


def _nsa(nq, kc, vct, ks, vs, kw, vw, misc, ovt, og):
    b, s, wq = nq.shape
    ts = NSA_SUB * NSA_TQ
    rows = NSA_HEADS * NSA_TQ

    def per_batch(a):
        return pl.BlockSpec((None,) + a.shape[1:], lambda i, j: (i, 0, 0))

    def per_tile(a):
        return pl.BlockSpec((None, ts, a.shape[2]), lambda i, j: (i, j, 0))

    def full(a):
        return pl.BlockSpec(a.shape, lambda i, j: (0, 0))

    return pl.pallas_call(
        _nsa_kernel,
        grid=(b, s // ts),
        in_specs=[per_tile(nq), per_batch(kc), per_batch(vct), per_batch(ks), per_batch(vs), per_batch(kw),
                  per_batch(vw), per_tile(misc), full(ovt), full(og)],
        out_specs=pl.BlockSpec((None, ts, wq), lambda i, j: (i, j, 0)),
        out_shape=jax.ShapeDtypeStruct((b, s, wq), BF16),
        scratch_shapes=[pltpu.VMEM((NSA_SUB, rows, 2 * NSA_HEAD_DIM), BF16),
                        pltpu.VMEM((NSA_SUB, rows, LANES), F32),
                        pltpu.VMEM((NSA_SUB, rows, 2 * NSA_HEAD_DIM), F32),
                        pltpu.VMEM((NSA_SUB, rows, NSA_HEAD_DIM), F32),
                        pltpu.VMEM((ts, N_BRANCH * NSA_HEADS * LANES), F32)],
        compiler_params=_cparams(("parallel", "arbitrary")),
        name="nsa",
    )(nq, kc, vct, ks, vs, kw, vw, misc, ovt, og)


def _gla_kernel(q_ref, k_ref, v_ref, og_ref, misc_ref, up_ref, bias_ref, g_ref, o_ref, st_ref):
    c = GLA_CHUNK
    kw = GLA_HEADS * GLA_KEY_DIM
    tg = q_ref.shape[1]
    n_chunk = tg // c
    log_c = int(math.log2(c))

    @pl.when(pl.program_id(1) == 0)
    def _():
        st_ref[...] = jnp.zeros_like(st_ref)

    ri = lax.broadcasted_iota(jnp.int32, (tg, tg), 0)
    ci = lax.broadcasted_iota(jnp.int32, (tg, tg), 1)
    low = ((ri >> log_c) == (ci >> log_c)) & (ci <= ri)
    causal = jnp.where(low, 1.0, 0.0)
    tri = causal.astype(BF16)
    lane_head = lax.broadcasted_iota(jnp.int32, (1, kw), 1) >> int(math.log2(GLA_KEY_DIM))
    head_mask = [lane_head == h for h in range(GLA_HEADS)]
    u_hi, u_lo = _split(up_ref[...])

    def stack_heads(x):
        return jnp.concatenate([jnp.where(head_mask[h], x, 0.0) for h in range(GLA_HEADS)], axis=0)

    def sequence(bb):
        m_hi, m_lo = _split(misc_ref[bb])
        g_pre = _dot(m_hi, u_hi) + _dot(m_lo, u_hi) + _dot(m_hi, u_lo) + bias_ref[...]
        yield
        g_log = (jnp.minimum(g_pre, 0.0) - jnp.log(1.0 + jnp.exp(-jnp.abs(g_pre)))) * (1.0 / GLA_GATE_NORM)
        g_hi, g_lo = _split(g_log)
        yield
        bcum = _dot(tri, g_hi) + _dot(tri, g_lo)
        yield
        b_last = jnp.concatenate([jnp.broadcast_to(bcum[(cc + 1) * c - 1:(cc + 1) * c, :], (c, kw))
                                  for cc in range(n_chunk)], axis=0)
        qf = q_ref[bb].astype(F32)
        kf = k_ref[bb].astype(F32)
        v = v_ref[bb]
        q_dec = qf * jnp.exp(bcum) * (GLA_KEY_DIM ** -0.5)
        k_dec = (kf * jnp.exp(-bcum)).astype(BF16)
        k_end = kf * jnp.exp(b_last - bcum)
        q_heads = [jnp.where(head_mask[h], q_dec, 0.0).astype(BF16) for h in range(GLA_HEADS)]
        yield
        scores = [_dot_nt(q_heads[h], k_dec) for h in range(GLA_HEADS)]
        yield
        a = [(scores[h] * causal).astype(BF16) for h in range(GLA_HEADS)]
        yield
        intra = [_dot(a[h], v[:, h * GLA_VAL_DIM:(h + 1) * GLA_VAL_DIM]) for h in range(GLA_HEADS)]
        st = st_ref[bb]
        inter = []
        for cc in range(n_chunk):
            r = slice(cc * c, (cc + 1) * c)
            inter.append(_dot_nt(stack_heads(q_dec[r]).astype(BF16), st.astype(BF16)))
            km = stack_heads(k_end[r]).astype(BF16)
            vst = jnp.concatenate([v[r, h * GLA_VAL_DIM:(h + 1) * GLA_VAL_DIM] for h in range(GLA_HEADS)],
                                  axis=0)
            st = st * jnp.exp(b_last[cc * c:cc * c + 1, :]) + _dot_tn(vst, km)
        st_ref[bb] = st
        yield
        for h in range(GLA_HEADS):
            vs = slice(h * GLA_VAL_DIM, (h + 1) * GLA_VAL_DIM)
            inter_h = jnp.concatenate([inter[cc][h * c:(h + 1) * c] for cc in range(n_chunk)], axis=0)
            gate = og_ref[bb, :, vs].astype(F32)
            o_ref[bb, :, vs] = (_rms(intra[h] + inter_h, g_ref[:, vs]) * (gate * _sigmoid(gate))).astype(BF16)
        yield

    _trace_skewed(sequence(bb) for bb in range(q_ref.shape[0]))


def _gla(gq, gk, gv, gog, misc, up_pad, bias, g):
    b, s, _ = gq.shape
    tg = min(GLA_TG, s)
    nb = GLA_NB if b % GLA_NB == 0 else 1

    def per_tile(a):
        return pl.BlockSpec((nb, tg, a.shape[2]), lambda i, j: (i, j, 0))

    def full(a):
        return pl.BlockSpec(a.shape, lambda i, j: (0, 0))

    return pl.pallas_call(
        _gla_kernel,
        grid=(b // nb, s // tg),
        in_specs=[per_tile(gq), per_tile(gk), per_tile(gv), per_tile(gog), per_tile(misc),
                  full(up_pad), full(bias), full(g)],
        out_specs=per_tile(gv),
        out_shape=jax.ShapeDtypeStruct(gv.shape, BF16),
        scratch_shapes=[pltpu.VMEM((nb, GLA_VAL_DIM, GLA_HEADS * GLA_KEY_DIM), F32)],
        compiler_params=_cparams(("parallel", "arbitrary")),
        name="gla",
    )(gq, gk, gv, gog, misc, up_pad, bias, g)


def _overlap_matrix(n_rows):
    n = np.arange(n_rows)[:, None] * CMP_STRIDE
    s = np.arange(LANES)[None, :] * SLC_BLOCK
    return ((n < s + SLC_BLOCK) & (n + CMP_BLOCK > s)).astype(np.float32)


def _layer(h, p, seq, batch, ffn1_pre_g, ffn1_post_g, ffn1_w_gate, ffn1_w_up, ffn1_w_down, mix_pre_g, mix_post_g,
           w_in, cmp_k_pe, cmp_k_w1, cmp_k_w2, cmp_v_pe, cmp_v_w1, cmp_v_w2, nsa_out_g, gla_gate_up, gla_gate_bias,
           gla_out_g, w_out, ffn2_pre_g, ffn2_post_g, ffn2_w_gate, ffn2_w_up, ffn2_w_down, ple_proj, ple_gate,
           ple_post_g):
    d = h.shape[1]
    row = lambda a: a.reshape(1, -1).astype(F32)
    bf = lambda a: a.astype(BF16)

    h1 = _ffn(h, row(ffn1_pre_g), row(ffn1_post_g), bf(ffn1_w_gate), bf(ffn1_w_up), bf(ffn1_w_down))

    o_gate = 512 + 6 * 128
    o_gq = o_gate + 3 * NSA_HEADS
    o_gk = o_gq + GLA_HEADS * GLA_KEY_DIM
    o_gv = o_gk + GLA_HEADS * GLA_KEY_DIM
    o_gd = o_gv + GLA_HEADS * GLA_VAL_DIM
    o_og = o_gd + GLA_GATE_RANK
    pad = jnp.zeros((d, LANES - 3 * NSA_HEADS - GLA_GATE_RANK), w_in.dtype)
    w_all = bf(jnp.concatenate([w_in[:, :o_gate], w_in[:, o_gate:o_gq], w_in[:, o_gd:o_og], pad,
                                w_in[:, o_gq:o_gd], w_in[:, o_og:]], axis=1))
    nq, kc_in, vc_in, ks, vs, kw, vw, misc, gq, gk, gv, gog = _proj(h1, row(mix_pre_g), w_all, seq)

    n_grp = seq // CMP_STRIDE
    grp_w = CMP_STRIDE * NSA_HEAD_DIM
    kc, vc = _compress(
        kc_in.reshape(batch, n_grp, grp_w), vc_in.reshape(batch, n_grp, grp_w),
        cmp_k_pe.reshape(2, grp_w).astype(F32), bf(cmp_k_w1.reshape(2 * grp_w, -1)), bf(cmp_k_w2),
        cmp_v_pe.reshape(2, grp_w).astype(F32), bf(cmp_v_w1.reshape(2 * grp_w, -1)), bf(cmp_v_w2))

    b3 = lambda a: a.reshape(batch, seq, a.shape[-1])
    ovt = jnp.asarray(_overlap_matrix(n_grp).T, BF16)
    o_nsa = _nsa(b3(nq), kc, vc, b3(ks), b3(vs), b3(kw), b3(vw), b3(misc), ovt, row(nsa_out_g))

    up_pad = jnp.zeros((LANES, GLA_HEADS * GLA_KEY_DIM), F32)
    up_pad = up_pad.at[_MISC_GDOWN:_MISC_GDOWN + GLA_GATE_RANK].set(gla_gate_up.astype(F32))
    o_gla = _gla(b3(gq), b3(gk), b3(gv), b3(gog), b3(misc), up_pad, row(gla_gate_bias), row(gla_out_g))

    w_o = bf(w_out)
    half = NSA_HEADS * NSA_HEAD_DIM
    return _ffn(h1, row(ffn2_pre_g), row(ffn2_post_g), bf(ffn2_w_gate), bf(ffn2_w_up), bf(ffn2_w_down),
                mix=(o_nsa.reshape(-1, half), o_gla.reshape(-1, half), w_o[:half], w_o[half:], row(mix_post_g)),
                ple=(p, bf(ple_proj), bf(ple_gate), row(ple_post_g)))


def kernel(x, p, ffn1_pre_g, ffn1_post_g, ffn1_w_gate, ffn1_w_up, ffn1_w_down, mix_pre_g, mix_post_g, w_in, cmp_k_pe, cmp_k_w1, cmp_k_w2, cmp_v_pe, cmp_v_w1, cmp_v_w2, nsa_out_g, gla_gate_up, gla_gate_bias, gla_out_g, w_out, ffn2_pre_g, ffn2_post_g, ffn2_w_gate, ffn2_w_up, ffn2_w_down, ple_proj, ple_gate, ple_post_g):
    batch, seq, d = x.shape
    depth = p.shape[0]
    assert seq % NSA_TK == 0 and seq >= WIN_SLAB and seq // SLC_BLOCK <= LANES
    h = x.reshape(batch * seq, d)
    params = (ffn1_pre_g, ffn1_post_g, ffn1_w_gate, ffn1_w_up, ffn1_w_down, mix_pre_g, mix_post_g, w_in, cmp_k_pe,
              cmp_k_w1, cmp_k_w2, cmp_v_pe, cmp_v_w1, cmp_v_w2, nsa_out_g, gla_gate_up, gla_gate_bias, gla_out_g,
              w_out, ffn2_pre_g, ffn2_post_g, ffn2_w_gate, ffn2_w_up, ffn2_w_down, ple_proj, ple_gate, ple_post_g)
    for i in range(depth):
        h = _layer(h, p[i].reshape(batch * seq, -1), seq, batch, *[a[i] for a in params])
    return h.reshape(batch, seq, d)
```

```python
import functools
import math

import numpy as np
import jax
import jax.numpy as jnp
from jax import lax
from jax.experimental import pallas as pl
from jax.experimental.pallas import tpu as pltpu

F32 = jnp.float32
BF16 = jnp.bfloat16

EPS = 1e-6
NSA_HEADS = 4
NSA_HEAD_DIM = 128
NSA_KV_W = 128
CMP_BLOCK = 32
CMP_STRIDE = 16
SLC_BLOCK = 64
SLC_TOPK = 16
WINDOW = 512
FORCE_BONUS = 1e4
NEG = -1e30
GLA_HEADS = 4
GLA_KEY_DIM = 64
GLA_VAL_DIM = 128
GLA_GATE_RANK = 16
GLA_GATE_NORM = 16.0
GLA_CHUNK = 64

LOG2E = 1.4426950408889634
N_FORCED = 3
N_BRANCH = 3

LANES = 128
VMEM_LIMIT = 56 * 1024 * 1024

FFN_TM = 1024
FFN_TF = 256
TOK_TM = 1024
NSA_TQ = 128
NSA_SUB = 4
NSA_TK = 512
NSA_UNROLL = 4
NSA_TKL = 2048
CMP_BUCKET = 128
GLA_TG = 256
GLA_NB = 8
WIN_SLAB = WINDOW + NSA_TQ


def _dot(a, b):
    return jnp.dot(a, b, preferred_element_type=F32)


def _dot_nt(a, b):
    return lax.dot_general(a, b, (((1,), (1,)), ((), ())), preferred_element_type=F32)


def _dot_tn(a, b):
    return lax.dot_general(a, b, (((0,), (0,)), ((), ())), preferred_element_type=F32)


def _split(x):
    hi = x.astype(BF16)
    lo = (x - hi.astype(F32)).astype(BF16)
    return hi, lo


def _rms(x, g):
    return x * lax.rsqrt(jnp.mean(x * x, axis=-1, keepdims=True) + EPS) * g


def _sigmoid(x):
    return 1.0 / (1.0 + jnp.exp(-x))


def _cparams(sem):
    return pltpu.CompilerParams(dimension_semantics=sem, vmem_limit_bytes=VMEM_LIMIT)


def _trace_skewed(stage_generators):
    pending = list(stage_generators)
    running = []
    while running or pending:
        if pending:
            running.append(pending.pop(0))
        for gen in list(running):
            if next(gen, "done") == "done":
                running.remove(gen)


def _ffn_kernel(n_mix, n_ple, *refs):
    h_ref, pre_ref, post_ref, wg_ref, wu_ref, wd_ref = refs[:6]
    mix_refs = refs[6:6 + n_mix]
    ple_refs = refs[6 + n_mix:6 + n_mix + n_ple]
    o_ref = refs[-1]
    h = h_ref[...]
    if n_mix:
        on_ref, ogl_ref, wa_ref, wb_ref, mg_ref = mix_refs
        h = h + _rms(_dot(on_ref[...], wa_ref[...]) + _dot(ogl_ref[...], wb_ref[...]), mg_ref[...])
    u = _rms(h, pre_ref[...]).astype(BF16)
    acc = None
    for c in range(wg_ref.shape[1] // FFN_TF):
        sl = slice(c * FFN_TF, (c + 1) * FFN_TF)
        a = _dot(u, wg_ref[:, sl])
        b = _dot(u, wu_ref[:, sl])
        part = _dot((a * _sigmoid(a) * b).astype(BF16), wd_ref[sl, :])
        acc = part if acc is None else acc + part
    h = h + 0.5 * _rms(acc, post_ref[...])
    if n_ple:
        p_ref, wp_ref, wgate_ref, pg_ref = ple_refs
        e = _dot(p_ref[...].astype(BF16), wp_ref[...])
        h = h + _rms(e * _sigmoid(_dot(h.astype(BF16), wgate_ref[...])), pg_ref[...])
    o_ref[...] = h


def _resident(a):
    return pl.BlockSpec(a.shape, lambda *_: (0,) * a.ndim, pipeline_mode=pl.Buffered(1))


def _ffn(h, pre_g, post_g, wg, wu, wd, mix=(), ple=()):
    t, d = h.shape
    tm = min(FFN_TM, t)

    def tile(a):
        return pl.BlockSpec((tm, a.shape[1]), lambda i: (i, 0))

    args = [h, pre_g, post_g, wg, wu, wd, *mix, *ple]
    specs = [tile(h)] + [_resident(a) for a in (pre_g, post_g, wg, wu, wd)]
    if mix:
        specs += [tile(mix[0]), tile(mix[1])] + [_resident(a) for a in mix[2:]]
    if ple:
        specs += [tile(ple[0])] + [_resident(a) for a in ple[1:]]
    return pl.pallas_call(
        functools.partial(_ffn_kernel, len(mix), len(ple)),
        grid=(t // tm,),
        in_specs=specs,
        out_specs=tile(h),
        out_shape=jax.ShapeDtypeStruct((t, d), F32),
        compiler_params=_cparams(("parallel",)),
        name="ffn",
    )(*args)


_P_NQ = (0, 512)
_P_KC = (512, 640)
_P_VC = (640, 768)
_P_KS = (768, 896)
_P_VS = (896, 1024)
_P_KW = (1024, 1152)
_P_VW = (1152, 1280)
_P_MISC = (1280, 1408)
_P_GQ = (1408, 1664)
_P_GK = (1664, 1920)
_P_GV = (1920, 2432)
_P_GOG = (2432, 2944)
_P_WIDTH = 2944
_MISC_GDOWN = 3 * NSA_HEADS


def _proj_kernel(seq, h_ref, g_ref, w_ref, nq_ref, kc_ref, vc_ref, ks_ref, vs_ref, kw_ref, vw_ref,
                 misc_ref, gq_ref, gk_ref, gv_ref, gog_ref, kc_tmp, vc_tmp):
    tm = h_ref.shape[0]
    u = _rms(h_ref[...], g_ref[...]).astype(BF16)

    def grp(lohi):
        return _dot(u, w_ref[:, lohi[0]:lohi[1]])

    nq_ref[...] = (grp(_P_NQ) * (NSA_HEAD_DIM ** -0.5 * LOG2E)).astype(BF16)
    kv_c = grp((_P_KC[0], _P_VC[1]))
    kc_tmp[...] = kv_c[:, 0:LANES]
    vc_tmp[...] = kv_c[:, LANES:2 * LANES]
    for l in range(CMP_STRIDE):
        rows_l = pl.ds(l, tm // CMP_STRIDE, stride=CMP_STRIDE)
        kc_ref[:, l * LANES:(l + 1) * LANES] = kc_tmp[rows_l, :].astype(BF16)
        vc_ref[:, l * LANES:(l + 1) * LANES] = vc_tmp[rows_l, :].astype(BF16)
    base = (pl.program_id(0) % (seq // tm)) * tm
    row = lax.broadcasted_iota(jnp.int32, (tm, LANES), 0)
    lane = lax.broadcasted_iota(jnp.int32, (tm, LANES), 1)
    onehot = jnp.where(lane == ((base + row) >> int(math.log2(SLC_BLOCK))), 1.0, 0.0).astype(BF16)
    kv_s = grp((_P_KS[0], _P_VS[1])).astype(BF16)
    ks_ref[:, 0:LANES] = kv_s[:, 0:LANES]
    ks_ref[:, LANES:2 * LANES] = onehot
    vs_ref[...] = kv_s[:, LANES:2 * LANES]
    kv_w = grp((_P_KW[0], _P_VW[1])).astype(BF16)
    kw_ref[...] = kv_w[:, 0:LANES]
    vw_ref[...] = kv_w[:, LANES:2 * LANES]
    misc_ref[...] = grp(_P_MISC)
    gq_ref[...] = grp(_P_GQ).astype(BF16)
    gk_ref[...] = grp(_P_GK).astype(BF16)
    gv_ref[...] = grp(_P_GV).astype(BF16)
    gog_ref[...] = grp(_P_GOG).astype(BF16)


def _proj(h, g, w_all, seq):
    t, d = h.shape
    tm = min(TOK_TM, seq)
    grp_w = CMP_STRIDE * NSA_KV_W
    outs = [(tm, 512, BF16), (tm // CMP_STRIDE, grp_w, BF16), (tm // CMP_STRIDE, grp_w, BF16), (tm, 256, BF16),
            (tm, 128, BF16), (tm, 128, BF16), (tm, 128, BF16), (tm, 128, F32), (tm, 256, BF16), (tm, 256, BF16),
            (tm, 512, BF16), (tm, 512, BF16)]
    return pl.pallas_call(
        functools.partial(_proj_kernel, seq),
        grid=(t // tm,),
        in_specs=[
            pl.BlockSpec((tm, d), lambda i: (i, 0)),
            pl.BlockSpec((1, d), lambda i: (0, 0)),
            pl.BlockSpec((d, _P_WIDTH), lambda i: (0, 0)),
        ],
        out_specs=[pl.BlockSpec((r, w), lambda i: (i, 0)) for r, w, _ in outs],
        out_shape=[jax.ShapeDtypeStruct((t // tm * r, w), dt) for r, w, dt in outs],
        scratch_shapes=[pltpu.VMEM((tm, NSA_KV_W), F32), pltpu.VMEM((tm, NSA_KV_W), F32)],
        compiler_params=_cparams(("parallel",)),
        name="proj",
    )(h, g, w_all)


def _compress_kernel(k_ref, v_ref, pek_ref, w1k_ref, w2k_ref, pev_ref, w1v_ref, w2v_ref, kc_ref, vc_ref):
    half = CMP_STRIDE * NSA_HEAD_DIM

    def one(x_ref, pe_ref, w1_ref, w2_ref):
        x = x_ref[...].astype(F32)
        n = x.shape[0]
        a = _dot((x + pe_ref[0:1, :]).astype(BF16), w1_ref[0:half, :])
        b = _dot((x + pe_ref[1:2, :]).astype(BF16), w1_ref[half:2 * half, :])
        hid = a + pltpu.roll(b, n - 1, 0)
        act = (hid * _sigmoid(hid)).astype(BF16)
        return _dot(act, w2_ref[...])

    kc_ref[...] = one(k_ref, pek_ref, w1k_ref, w2k_ref).astype(BF16)
    vc_ref[...] = one(v_ref, pev_ref, w1v_ref, w2v_ref).T.astype(BF16)


def _compress(k2, v2, pek, w1k, w2k, pev, w1v, w2v):
    b, n, w = k2.shape
    dk = w2k.shape[1]
    big = pl.BlockSpec((None, n, w), lambda i: (i, 0, 0))

    def full(a):
        return pl.BlockSpec(a.shape, lambda i: (0,) * a.ndim)

    return pl.pallas_call(
        _compress_kernel,
        grid=(b,),
        in_specs=[big, big, full(pek), full(w1k), full(w2k), full(pev), full(w1v), full(w2v)],
        out_specs=[pl.BlockSpec((None, n, dk), lambda i: (i, 0, 0)), pl.BlockSpec((None, dk, n), lambda i: (i, 0, 0))],
        out_shape=[jax.ShapeDtypeStruct((b, n, dk), BF16), jax.ShapeDtypeStruct((b, dk, n), BF16)],
        compiler_params=_cparams(("parallel",)),
        name="compress",
    )(k2, v2, pek, w1k, w2k, pev, w1v, w2v)


def _nsa_kernel(q_ref, kc_ref, vct_ref, ks_ref, vs_ref, kw_ref, vw_ref, misc_ref, ovt_ref, og_ref, o_ref,
                qaug_ref, m_ref, acc_ref, oc_ref, gate_ref):
    tq = NSA_TQ
    rows = NSA_HEADS * tq
    jd = pl.program_id(1)
    step0 = jd * NSA_TK
    n_c = kc_ref.shape[0]
    log_g = int(math.log2(CMP_STRIDE))
    row_i = lax.broadcasted_iota(jnp.int32, (rows, LANES), 0) & (tq - 1)
    lane = lax.broadcasted_iota(jnp.int32, (rows, LANES), 1)
    lane_q = lax.broadcasted_iota(jnp.int32, (1, rows), 1) & (tq - 1)
    tri_diag = jnp.where(lane <= row_i, 0.0, NEG)
    tri_far = jnp.where(lane > row_i, 0.0, NEG)
    q_grp = jnp.where(lane == ((row_i + 1) >> log_g), 1.0, 0.0).astype(BF16)
    n_minus_g = (lax.broadcasted_iota(jnp.int32, (n_c, LANES), 0)
                 - lax.broadcasted_iota(jnp.int32, (n_c, LANES), 1))

    def spread_gates(r):
        n_gate = N_BRANCH * NSA_HEADS
        spread = jnp.where(lax.broadcasted_iota(jnp.int32, (LANES, n_gate * LANES), 0)
                           == (lax.broadcasted_iota(jnp.int32, (LANES, n_gate * LANES), 1) >> int(math.log2(LANES))),
                           1.0, 0.0).astype(BF16)
        yield
        g_hi, g_lo = _split(_sigmoid(misc_ref[r * tq:(r + 1) * tq, :]))
        yield
        gate_ref[r * tq:(r + 1) * tq, :] = _dot(g_hi, spread) + _dot(g_lo, spread)
        yield

    def heads_to_rows(q):
        return jnp.concatenate([q[:, h * NSA_HEAD_DIM:(h + 1) * NSA_HEAD_DIM] for h in range(NSA_HEADS)], axis=0)

    def prepare(r, n_vis):
        n_blk = n_vis * CMP_STRIDE // SLC_BLOCK
        q0 = step0 + r * tq
        qh = heads_to_rows(q_ref[r * tq:(r + 1) * tq, :])
        vis = jnp.where(n_minus_g[0:n_vis] <= (q0 >> log_g) - 2, 0.0, NEG).astype(BF16)
        st = _dot_nt(jnp.concatenate([kc_ref[0:n_vis, :], vis], axis=1), jnp.concatenate([qh, q_grp], axis=1))
        yield
        et = jnp.exp2(st - jnp.max(st, axis=0, keepdims=True))
        inv = jnp.where(q0 + lane_q >= CMP_BLOCK - 1, 1.0 / jnp.sum(et, axis=0, keepdims=True), 0.0)
        p_t = et * inv
        p_bf = p_t.astype(BF16)
        p_sum = p_t[:, 0:tq]
        for h in range(1, NSA_HEADS):
            p_sum = p_sum + p_t[:, h * tq:(h + 1) * tq]
        p_hi, p_lo = _split(p_sum)
        yield
        oc_ref[r] = _dot(vct_ref[:, 0:n_vis], p_bf).T
        ovt = ovt_ref[0:n_blk, 0:n_vis]
        imp = _dot(ovt, p_hi) + _dot(ovt, p_lo)
        yield
        blk = lax.broadcasted_iota(jnp.int32, (n_blk, tq), 0)
        cur = (q0 + lax.broadcasted_iota(jnp.int32, (n_blk, tq), 1)) >> int(math.log2(SLC_BLOCK))
        forced = (blk == 0) | (blk == cur) | (blk == cur - 1)
        val = jnp.where(forced | (blk > cur), -1.0, imp)
        blk_f = blk.astype(F32)
        for _ in range(SLC_TOPK - N_FORCED):
            top = jnp.max(val, axis=0, keepdims=True)
            first = jnp.min(jnp.where(val == top, blk_f, float(LANES)), axis=0, keepdims=True)
            val = jnp.where(blk_f == first, -2.0, val)
        bias_t = jnp.where(forced | (val == -2.0), 0.0, NEG)
        if n_blk < LANES:
            bias_t = jnp.concatenate([bias_t, jnp.full((LANES - n_blk, tq), NEG, F32)], axis=0)
        bias = bias_t.T.astype(BF16)
        qaug_ref[r] = jnp.concatenate([qh, jnp.concatenate([bias] * NSA_HEADS, axis=0)], axis=1)
        m_ref[r] = jnp.full((rows, LANES), NEG, F32)
        acc_ref[r] = jnp.zeros((rows, 2 * NSA_HEAD_DIM), F32)
        yield

    bucket_rows = min(CMP_BUCKET, n_c)
    steps_per_bucket = bucket_rows * CMP_STRIDE // NSA_TK
    for bucket in range(n_c // bucket_rows):
        @pl.when(jd // steps_per_bucket == bucket)
        def _(bucket=bucket):
            _trace_skewed([prepare(r, (bucket + 1) * bucket_rows) for r in range(NSA_SUB)]
                          + [spread_gates(r) for r in range(NSA_SUB)])

    def mask_last_chunk(sc):
        w = sc.shape[1]
        if w == LANES:
            return sc + tri_diag
        return jnp.concatenate([sc[:, :w - LANES], sc[:, w - LANES:] + tri_diag], axis=1)

    def with_ones(v):
        return jnp.concatenate([v, jnp.ones(v.shape, BF16)], axis=1)

    def flash(r, k0, w):
        sc = _dot_nt(qaug_ref[r], ks_ref[pl.ds(k0, w), :])
        yield
        m_i = m_ref[r]
        m_n = jnp.maximum(m_i, jnp.max(sc, axis=1, keepdims=True))
        alpha = jnp.exp2(m_i - m_n)
        p = jnp.exp2(sc - jnp.concatenate([m_n] * (w // LANES), axis=1)).astype(BF16)
        yield
        acc_ref[r] = (jnp.concatenate([alpha, alpha], axis=1) * acc_ref[r]
                      + _dot(p, with_ones(vs_ref[pl.ds(k0, w), :])))
        m_ref[r] = m_n
        yield

    def full_tiles(j0, n):
        spans = [(t, min(NSA_TKL // NSA_TK, n - t)) for t in range(0, n, NSA_TKL // NSA_TK)]
        _trace_skewed(flash(r, pl.multiple_of((j0 + t) * NSA_TK, NSA_TK), cnt * NSA_TK)
                      for t, cnt in spans for r in range(NSA_SUB))

    def tile_group(j, carry):
        full_tiles(j * NSA_UNROLL, NSA_UNROLL)
        return carry

    lax.fori_loop(0, jd // NSA_UNROLL, tile_group, 0)
    for rem in range(1, NSA_UNROLL):
        @pl.when(jd % NSA_UNROLL == rem)
        def _(rem=rem):
            full_tiles(jd - rem, rem)

    def finish(r, first_step):
        w = (r + 1) * tq
        d0 = pl.multiple_of(step0, NSA_TK)
        sc = mask_last_chunk(_dot_nt(qaug_ref[r], ks_ref[pl.ds(d0, w), :]))
        vt = vs_ref[pl.ds(d0, w), :]
        qh = qaug_ref[r][:, 0:NSA_HEAD_DIM]
        if first_step:
            sw = mask_last_chunk(_dot_nt(qh, kw_ref[0:w, :]))
            vw = vw_ref[0:w, :]
        else:
            w0 = pl.multiple_of(step0 + r * tq - WINDOW, tq)
            sw = _dot_nt(qh, kw_ref[pl.ds(w0, WIN_SLAB), :])
            sw = jnp.concatenate([sw[:, :LANES] + tri_far, sw[:, LANES:WIN_SLAB - LANES],
                                  sw[:, WIN_SLAB - LANES:] + tri_diag], axis=1)
            vw = vw_ref[pl.ds(w0, WIN_SLAB), :]
        yield
        m_i = m_ref[r]
        m_n = jnp.maximum(m_i, jnp.max(sc, axis=1, keepdims=True))
        alpha = jnp.exp2(m_i - m_n)
        p = jnp.exp2(sc - jnp.concatenate([m_n] * (w // LANES), axis=1)).astype(BF16)
        ew = jnp.exp2(sw - jnp.max(sw, axis=1, keepdims=True)).astype(BF16)
        yield
        acc = jnp.concatenate([alpha, alpha], axis=1) * acc_ref[r] + _dot(p, with_ones(vt))
        pv = _dot(ew, with_ones(vw))
        yield
        o_s = acc[:, 0:NSA_HEAD_DIM] * (1.0 / acc[:, NSA_HEAD_DIM:])
        o_w = pv[:, 0:NSA_HEAD_DIM] * (1.0 / pv[:, NSA_HEAD_DIM:])
        o_c = oc_ref[r]
        outs = []
        for h in range(NSA_HEADS):
            hr = slice(h * tq, (h + 1) * tq)
            gc, gs, gw = (gate_ref[r * tq:(r + 1) * tq,
                                   (N_BRANCH * h + k) * LANES:(N_BRANCH * h + k + 1) * LANES]
                          for k in range(N_BRANCH))
            o_h = gc * o_c[hr] + gs * o_s[hr] + gw * o_w[hr]
            outs.append(_rms(o_h, og_ref[:, h * NSA_HEAD_DIM:(h + 1) * NSA_HEAD_DIM]))
        o_ref[r * tq:(r + 1) * tq, :] = jnp.concatenate(outs, axis=1).astype(BF16)
        yield

    @pl.when(jd == 0)
    def _():
        _trace_skewed(finish(r, True) for r in range(NSA_SUB))

    @pl.when(jd > 0)
    def _():
        _trace_skewed(finish(r, False) for r in range(NSA_SUB))


def _nsa(nq, kc, vct, ks, vs, kw, vw, misc, ovt, og):
    b, s, wq = nq.shape
    ts = NSA_SUB * NSA_TQ
    rows = NSA_HEADS * NSA_TQ

    def per_batch(a):
        return pl.BlockSpec((None,) + a.shape[1:], lambda i, j: (i, 0, 0))

    def per_tile(a):
        return pl.BlockSpec((None, ts, a.shape[2]), lambda i, j: (i, j, 0))

    def full(a):
        return pl.BlockSpec(a.shape, lambda i, j: (0, 0))

    return pl.pallas_call(
        _nsa_kernel,
        grid=(b, s // ts),
        in_specs=[per_tile(nq), per_batch(kc), per_batch(vct), per_batch(ks), per_batch(vs), per_batch(kw),
                  per_batch(vw), per_tile(misc), full(ovt), full(og)],
        out_specs=pl.BlockSpec((None, ts, wq), lambda i, j: (i, j, 0)),
        out_shape=jax.ShapeDtypeStruct((b, s, wq), BF16),
        scratch_shapes=[pltpu.VMEM((NSA_SUB, rows, 2 * NSA_HEAD_DIM), BF16),
                        pltpu.VMEM((NSA_SUB, rows, LANES), F32),
                        pltpu.VMEM((NSA_SUB, rows, 2 * NSA_HEAD_DIM), F32),
                        pltpu.VMEM((NSA_SUB, rows, NSA_HEAD_DIM), F32),
                        pltpu.VMEM((ts, N_BRANCH * NSA_HEADS * LANES), F32)],
        compiler_params=_cparams(("parallel", "arbitrary")),
        name="nsa",
    )(nq, kc, vct, ks, vs, kw, vw, misc, ovt, og)


def _gla_kernel(q_ref, k_ref, v_ref, og_ref, misc_ref, up_ref, bias_ref, g_ref, o_ref, st_ref):
    c = GLA_CHUNK
    kw = GLA_HEADS * GLA_KEY_DIM
    tg = q_ref.shape[1]
    n_chunk = tg // c
    log_c = int(math.log2(c))

    @pl.when(pl.program_id(1) == 0)
    def _():
        st_ref[...] = jnp.zeros_like(st_ref)

    ri = lax.broadcasted_iota(jnp.int32, (tg, tg), 0)
    ci = lax.broadcasted_iota(jnp.int32, (tg, tg), 1)
    low = ((ri >> log_c) == (ci >> log_c)) & (ci <= ri)
    causal = jnp.where(low, 1.0, 0.0)
    tri = causal.astype(BF16)
    lane_head = lax.broadcasted_iota(jnp.int32, (1, kw), 1) >> int(math.log2(GLA_KEY_DIM))
    head_mask = [lane_head == h for h in range(GLA_HEADS)]
    u_hi, u_lo = _split(up_ref[...])

    def stack_heads(x):
        return jnp.concatenate([jnp.where(head_mask[h], x, 0.0) for h in range(GLA_HEADS)], axis=0)

    def sequence(bb):
        m_hi, m_lo = _split(misc_ref[bb])
        g_pre = _dot(m_hi, u_hi) + _dot(m_lo, u_hi) + _dot(m_hi, u_lo) + bias_ref[...]
        yield
        g_log = (jnp.minimum(g_pre, 0.0) - jnp.log(1.0 + jnp.exp(-jnp.abs(g_pre)))) * (1.0 / GLA_GATE_NORM)
        g_hi, g_lo = _split(g_log)
        yield
        bcum = _dot(tri, g_hi) + _dot(tri, g_lo)
        yield
        b_last = jnp.concatenate([jnp.broadcast_to(bcum[(cc + 1) * c - 1:(cc + 1) * c, :], (c, kw))
                                  for cc in range(n_chunk)], axis=0)
        qf = q_ref[bb].astype(F32)
        kf = k_ref[bb].astype(F32)
        v = v_ref[bb]
        q_dec = qf * jnp.exp(bcum) * (GLA_KEY_DIM ** -0.5)
        k_dec = (kf * jnp.exp(-bcum)).astype(BF16)
        k_end = kf * jnp.exp(b_last - bcum)
        q_heads = [jnp.where(head_mask[h], q_dec, 0.0).astype(BF16) for h in range(GLA_HEADS)]
        yield
        scores = [_dot_nt(q_heads[h], k_dec) for h in range(GLA_HEADS)]
        yield
        a = [(scores[h] * causal).astype(BF16) for h in range(GLA_HEADS)]
        yield
        intra = [_dot(a[h], v[:, h * GLA_VAL_DIM:(h + 1) * GLA_VAL_DIM]) for h in range(GLA_HEADS)]
        st = st_ref[bb]
        inter = []
        for cc in range(n_chunk):
            r = slice(cc * c, (cc + 1) * c)
            inter.append(_dot_nt(stack_heads(q_dec[r]).astype(BF16), st.astype(BF16)))
            km = stack_heads(k_end[r]).astype(BF16)
            vst = jnp.concatenate([v[r, h * GLA_VAL_DIM:(h + 1) * GLA_VAL_DIM] for h in range(GLA_HEADS)],
                                  axis=0)
            st = st * jnp.exp(b_last[cc * c:cc * c + 1, :]) + _dot_tn(vst, km)
        st_ref[bb] = st
        yield
        for h in range(GLA_HEADS):
            vs = slice(h * GLA_VAL_DIM, (h + 1) * GLA_VAL_DIM)
            inter_h = jnp.concatenate([inter[cc][h * c:(h + 1) * c] for cc in range(n_chunk)], axis=0)
            gate = og_ref[bb, :, vs].astype(F32)
            o_ref[bb, :, vs] = (_rms(intra[h] + inter_h, g_ref[:, vs]) * (gate * _sigmoid(gate))).astype(BF16)
        yield

    _trace_skewed(sequence(bb) for bb in range(q_ref.shape[0]))


def _gla(gq, gk, gv, gog, misc, up_pad, bias, g):
    b, s, _ = gq.shape
    tg = min(GLA_TG, s)
    nb = GLA_NB if b % GLA_NB == 0 else 1

    def per_tile(a):
        return pl.BlockSpec((nb, tg, a.shape[2]), lambda i, j: (i, j, 0))

    def full(a):
        return pl.BlockSpec(a.shape, lambda i, j: (0, 0))

    return pl.pallas_call(
        _gla_kernel,
        grid=(b // nb, s // tg),
        in_specs=[per_tile(gq), per_tile(gk), per_tile(gv), per_tile(gog), per_tile(misc),
                  full(up_pad), full(bias), full(g)],
        out_specs=per_tile(gv),
        out_shape=jax.ShapeDtypeStruct(gv.shape, BF16),
        scratch_shapes=[pltpu.VMEM((nb, GLA_VAL_DIM, GLA_HEADS * GLA_KEY_DIM), F32)],
        compiler_params=_cparams(("parallel", "arbitrary")),
        name="gla",
    )(gq, gk, gv, gog, misc, up_pad, bias, g)


def _overlap_matrix(n_rows):
    n = np.arange(n_rows)[:, None] * CMP_STRIDE
    s = np.arange(LANES)[None, :] * SLC_BLOCK
    return ((n < s + SLC_BLOCK) & (n + CMP_BLOCK > s)).astype(np.float32)


def _layer(h, p, seq, batch, ffn1_pre_g, ffn1_post_g, ffn1_w_gate, ffn1_w_up, ffn1_w_down, mix_pre_g, mix_post_g,
           w_in, cmp_k_pe, cmp_k_w1, cmp_k_w2, cmp_v_pe, cmp_v_w1, cmp_v_w2, nsa_out_g, gla_gate_up, gla_gate_bias,
           gla_out_g, w_out, ffn2_pre_g, ffn2_post_g, ffn2_w_gate, ffn2_w_up, ffn2_w_down, ple_proj, ple_gate,
           ple_post_g):
    d = h.shape[1]
    row = lambda a: a.reshape(1, -1).astype(F32)
    bf = lambda a: a.astype(BF16)

    h1 = _ffn(h, row(ffn1_pre_g), row(ffn1_post_g), bf(ffn1_w_gate), bf(ffn1_w_up), bf(ffn1_w_down))

    o_gate = 512 + 6 * 128
    o_gq = o_gate + 3 * NSA_HEADS
    o_gk = o_gq + GLA_HEADS * GLA_KEY_DIM
    o_gv = o_gk + GLA_HEADS * GLA_KEY_DIM
    o_gd = o_gv + GLA_HEADS * GLA_VAL_DIM
    o_og = o_gd + GLA_GATE_RANK
    pad = jnp.zeros((d, LANES - 3 * NSA_HEADS - GLA_GATE_RANK), w_in.dtype)
    w_all = bf(jnp.concatenate([w_in[:, :o_gate], w_in[:, o_gate:o_gq], w_in[:, o_gd:o_og], pad,
                                w_in[:, o_gq:o_gd], w_in[:, o_og:]], axis=1))
    nq, kc_in, vc_in, ks, vs, kw, vw, misc, gq, gk, gv, gog = _proj(h1, row(mix_pre_g), w_all, seq)

    n_grp = seq // CMP_STRIDE
    grp_w = CMP_STRIDE * NSA_HEAD_DIM
    kc, vc = _compress(
        kc_in.reshape(batch, n_grp, grp_w), vc_in.reshape(batch, n_grp, grp_w),
        cmp_k_pe.reshape(2, grp_w).astype(F32), bf(cmp_k_w1.reshape(2 * grp_w, -1)), bf(cmp_k_w2),
        cmp_v_pe.reshape(2, grp_w).astype(F32), bf(cmp_v_w1.reshape(2 * grp_w, -1)), bf(cmp_v_w2))

    b3 = lambda a: a.reshape(batch, seq, a.shape[-1])
    ovt = jnp.asarray(_overlap_matrix(n_grp).T, BF16)
    o_nsa = _nsa(b3(nq), kc, vc, b3(ks), b3(vs), b3(kw), b3(vw), b3(misc), ovt, row(nsa_out_g))

    up_pad = jnp.zeros((LANES, GLA_HEADS * GLA_KEY_DIM), F32)
    up_pad = up_pad.at[_MISC_GDOWN:_MISC_GDOWN + GLA_GATE_RANK].set(gla_gate_up.astype(F32))
    o_gla = _gla(b3(gq), b3(gk), b3(gv), b3(gog), b3(misc), up_pad, row(gla_gate_bias), row(gla_out_g))

    w_o = bf(w_out)
    half = NSA_HEADS * NSA_HEAD_DIM
    return _ffn(h1, row(ffn2_pre_g), row(ffn2_post_g), bf(ffn2_w_gate), bf(ffn2_w_up), bf(ffn2_w_down),
                mix=(o_nsa.reshape(-1, half), o_gla.reshape(-1, half), w_o[:half], w_o[half:], row(mix_post_g)),
                ple=(p, bf(ple_proj), bf(ple_gate), row(ple_post_g)))


def kernel(x, p, ffn1_pre_g, ffn1_post_g, ffn1_w_gate, ffn1_w_up, ffn1_w_down, mix_pre_g, mix_post_g, w_in, cmp_k_pe, cmp_k_w1, cmp_k_w2, cmp_v_pe, cmp_v_w1, cmp_v_w2, nsa_out_g, gla_gate_up, gla_gate_bias, gla_out_g, w_out, ffn2_pre_g, ffn2_post_g, ffn2_w_gate, ffn2_w_up, ffn2_w_down, ple_proj, ple_gate, ple_post_g):
    batch, seq, d = x.shape
    depth = p.shape[0]
    assert seq % NSA_TK == 0 and seq >= WIN_SLAB and seq // SLC_BLOCK <= LANES
    h = x.reshape(batch * seq, d)
    params = (ffn1_pre_g, ffn1_post_g, ffn1_w_gate, ffn1_w_up, ffn1_w_down, mix_pre_g, mix_post_g, w_in, cmp_k_pe,
              cmp_k_w1, cmp_k_w2, cmp_v_pe, cmp_v_w1, cmp_v_w2, nsa_out_g, gla_gate_up, gla_gate_bias, gla_out_g,
              w_out, ffn2_pre_g, ffn2_post_g, ffn2_w_gate, ffn2_w_up, ffn2_w_down, ple_proj, ple_gate, ple_post_g)
    for i in range(depth):
        h = _layer(h, p[i].reshape(batch * seq, -1), seq, batch, *[a[i] for a in params])
    return h.reshape(batch, seq, d)
```

```python
import functools
import math

import numpy as np
import jax
import jax.numpy as jnp
from jax import lax
from jax.experimental import pallas as pl
from jax.experimental.pallas import tpu as pltpu

F32 = jnp.float32
BF16 = jnp.bfloat16

EPS = 1e-6
NSA_HEADS = 4
NSA_HEAD_DIM = 128
NSA_KV_W = 128
CMP_BLOCK = 32
CMP_STRIDE = 16
SLC_BLOCK = 64
SLC_TOPK = 16
WINDOW = 512
FORCE_BONUS = 1e4
NEG = -1e30
GLA_HEADS = 4
GLA_KEY_DIM = 64
GLA_VAL_DIM = 128
GLA_GATE_RANK = 16
GLA_GATE_NORM = 16.0
GLA_CHUNK = 64

LOG2E = 1.4426950408889634
N_FORCED = 3
N_BRANCH = 3

LANES = 128
VMEM_LIMIT = 56 * 1024 * 1024

FFN_TM = 1024
FFN_TF = 256
TOK_TM = 1024
NSA_TQ = 128
NSA_SUB = 4
NSA_TK = 512
NSA_UNROLL = 4
NSA_TKL = 2048
CMP_BUCKET = 128
GLA_TG = 256
GLA_NB = 8
WIN_SLAB = WINDOW + NSA_TQ


def _dot(a, b):
    return jnp.dot(a, b, preferred_element_type=F32)


def _dot_nt(a, b):
    return lax.dot_general(a, b, (((1,), (1,)), ((), ())), preferred_element_type=F32)


def _dot_tn(a, b):
    return lax.dot_general(a, b, (((0,), (0,)), ((), ())), preferred_element_type=F32)


def _split(x):
    hi = x.astype(BF16)
    lo = (x - hi.astype(F32)).astype(BF16)
    return hi, lo


def _rms(x, g):
    return x * lax.rsqrt(jnp.mean(x * x, axis=-1, keepdims=True) + EPS) * g


def _sigmoid(x):
    return 1.0 / (1.0 + jnp.exp(-x))


def _cparams(sem):
    return pltpu.CompilerParams(dimension_semantics=sem, vmem_limit_bytes=VMEM_LIMIT)


def _trace_skewed(stage_generators):
    pending = list(stage_generators)
    running = []
    while running or pending:
        if pending:
            running.append(pending.pop(0))
        for gen in list(running):
            if next(gen, "done") == "done":
                running.remove(gen)


def _ffn_kernel(n_mix, n_ple, *refs):
    h_ref, pre_ref, post_ref, wg_ref, wu_ref, wd_ref = refs[:6]
    mix_refs = refs[6:6 + n_mix]
    ple_refs = refs[6 + n_mix:6 + n_mix + n_ple]
    o_ref = refs[-1]
    h = h_ref[...]
    if n_mix:
        on_ref, ogl_ref, wa_ref, wb_ref, mg_ref = mix_refs
        h = h + _rms(_dot(on_ref[...], wa_ref[...]) + _dot(ogl_ref[...], wb_ref[...]), mg_ref[...])
    u = _rms(h, pre_ref[...]).astype(BF16)
    acc = None
    for c in range(wg_ref.shape[1] // FFN_TF):
        sl = slice(c * FFN_TF, (c + 1) * FFN_TF)
        a = _dot(u, wg_ref[:, sl])
        b = _dot(u, wu_ref[:, sl])
        part = _dot((a * _sigmoid(a) * b).astype(BF16), wd_ref[sl, :])
        acc = part if acc is None else acc + part
    h = h + 0.5 * _rms(acc, post_ref[...])
    if n_ple:
        p_ref, wp_ref, wgate_ref, pg_ref = ple_refs
        e = _dot(p_ref[...].astype(BF16), wp_ref[...])
        h = h + _rms(e * _sigmoid(_dot(h.astype(BF16), wgate_ref[...])), pg_ref[...])
    o_ref[...] = h


def _resident(a):
    return pl.BlockSpec(a.shape, lambda *_: (0,) * a.ndim, pipeline_mode=pl.Buffered(1))


def _ffn(h, pre_g, post_g, wg, wu, wd, mix=(), ple=()):
    t, d = h.shape
    tm = min(FFN_TM, t)

    def tile(a):
        return pl.BlockSpec((tm, a.shape[1]), lambda i: (i, 0))

    args = [h, pre_g, post_g, wg, wu, wd, *mix, *ple]
    specs = [tile(h)] + [_resident(a) for a in (pre_g, post_g, wg, wu, wd)]
    if mix:
        specs += [tile(mix[0]), tile(mix[1])] + [_resident(a) for a in mix[2:]]
    if ple:
        specs += [tile(ple[0])] + [_resident(a) for a in ple[1:]]
    return pl.pallas_call(
        functools.partial(_ffn_kernel, len(mix), len(ple)),
        grid=(t // tm,),
        in_specs=specs,
        out_specs=tile(h),
        out_shape=jax.ShapeDtypeStruct((t, d), F32),
        compiler_params=_cparams(("parallel",)),
        name="ffn",
    )(*args)


_P_NQ = (0, 512)
_P_KC = (512, 640)
_P_VC = (640, 768)
_P_KS = (768, 896)
_P_VS = (896, 1024)
_P_KW = (1024, 1152)
_P_VW = (1152, 1280)
_P_MISC = (1280, 1408)
_P_GQ = (1408, 1664)
_P_GK = (1664, 1920)
_P_GV = (1920, 2432)
_P_GOG = (2432, 2944)
_P_WIDTH = 2944
_MISC_GDOWN = 3 * NSA_HEADS


def _proj_kernel(seq, h_ref, g_ref, w_ref, nq_ref, kc_ref, vc_ref, ks_ref, vs_ref, kw_ref, vw_ref,
                 misc_ref, gq_ref, gk_ref, gv_ref, gog_ref, kc_tmp, vc_tmp):
    tm = h_ref.shape[0]
    u = _rms(h_ref[...], g_ref[...]).astype(BF16)

    def grp(lohi):
        return _dot(u, w_ref[:, lohi[0]:lohi[1]])

    nq_ref[...] = (grp(_P_NQ) * (NSA_HEAD_DIM ** -0.5 * LOG2E)).astype(BF16)
    kv_c = grp((_P_KC[0], _P_VC[1]))
    kc_tmp[...] = kv_c[:, 0:LANES]
    vc_tmp[...] = kv_c[:, LANES:2 * LANES]
    for l in range(CMP_STRIDE):
        rows_l = pl.ds(l, tm // CMP_STRIDE, stride=CMP_STRIDE)
        kc_ref[:, l * LANES:(l + 1) * LANES] = kc_tmp[rows_l, :].astype(BF16)
        vc_ref[:, l * LANES:(l + 1) * LANES] = vc_tmp[rows_l, :].astype(BF16)
    base = (pl.program_id(0) % (seq // tm)) * tm
    row = lax.broadcasted_iota(jnp.int32, (tm, LANES), 0)
    lane = lax.broadcasted_iota(jnp.int32, (tm, LANES), 1)
    onehot = jnp.where(lane == ((base + row) >> int(math.log2(SLC_BLOCK))), 1.0, 0.0).astype(BF16)
    kv_s = grp((_P_KS[0], _P_VS[1])).astype(BF16)
    ks_ref[:, 0:LANES] = kv_s[:, 0:LANES]
    ks_ref[:, LANES:2 * LANES] = onehot
    vs_ref[...] = kv_s[:, LANES:2 * LANES]
    kv_w = grp((_P_KW[0], _P_VW[1])).astype(BF16)
    kw_ref[...] = kv_w[:, 0:LANES]
    vw_ref[...] = kv_w[:, LANES:2 * LANES]
    misc_ref[...] = grp(_P_MISC)
    gq_ref[...] = grp(_P_GQ).astype(BF16)
    gk_ref[...] = grp(_P_GK).astype(BF16)
    gv_ref[...] = grp(_P_GV).astype(BF16)
    gog_ref[...] = grp(_P_GOG).astype(BF16)


def _proj(h, g, w_all, seq):
    t, d = h.shape
    tm = min(TOK_TM, seq)
    grp_w = CMP_STRIDE * NSA_KV_W
    outs = [(tm, 512, BF16), (tm // CMP_STRIDE, grp_w, BF16), (tm // CMP_STRIDE, grp_w, BF16), (tm, 256, BF16),
            (tm, 128, BF16), (tm, 128, BF16), (tm, 128, BF16), (tm, 128, F32), (tm, 256, BF16), (tm, 256, BF16),
            (tm, 512, BF16), (tm, 512, BF16)]
    return pl.pallas_call(
        functools.partial(_proj_kernel, seq),
        grid=(t // tm,),
        in_specs=[
            pl.BlockSpec((tm, d), lambda i: (i, 0)),
            pl.BlockSpec((1, d), lambda i: (0, 0)),
            pl.BlockSpec((d, _P_WIDTH), lambda i: (0, 0)),
        ],
        out_specs=[pl.BlockSpec((r, w), lambda i: (i, 0)) for r, w, _ in outs],
        out_shape=[jax.ShapeDtypeStruct((t // tm * r, w), dt) for r, w, dt in outs],
        scratch_shapes=[pltpu.VMEM((tm, NSA_KV_W), F32), pltpu.VMEM((tm, NSA_KV_W), F32)],
        compiler_params=_cparams(("parallel",)),
        name="proj",
    )(h, g, w_all)


def _compress_kernel(k_ref, v_ref, pek_ref, w1k_ref, w2k_ref, pev_ref, w1v_ref, w2v_ref, kc_ref, vc_ref):
    half = CMP_STRIDE * NSA_HEAD_DIM

    def one(x_ref, pe_ref, w1_ref, w2_ref):
        x = x_ref[...].astype(F32)
        n = x.shape[0]
        a = _dot((x + pe_ref[0:1, :]).astype(BF16), w1_ref[0:half, :])
        b = _dot((x + pe_ref[1:2, :]).astype(BF16), w1_ref[half:2 * half, :])
        hid = a + pltpu.roll(b, n - 1, 0)
        act = (hid * _sigmoid(hid)).astype(BF16)
        return _dot(act, w2_ref[...])

    kc_ref[...] = one(k_ref, pek_ref, w1k_ref, w2k_ref).astype(BF16)
    vc_ref[...] = one(v_ref, pev_ref, w1v_ref, w2v_ref).T.astype(BF16)


def _compress(k2, v2, pek, w1k, w2k, pev, w1v, w2v):
    b, n, w = k2.shape
    dk = w2k.shape[1]
    big = pl.BlockSpec((None, n, w), lambda i: (i, 0, 0))

    def full(a):
        return pl.BlockSpec(a.shape, lambda i: (0,) * a.ndim)

    return pl.pallas_call(
        _compress_kernel,
        grid=(b,),
        in_specs=[big, big, full(pek), full(w1k), full(w2k), full(pev), full(w1v), full(w2v)],
        out_specs=[pl.BlockSpec((None, n, dk), lambda i: (i, 0, 0)), pl.BlockSpec((None, dk, n), lambda i: (i, 0, 0))],
        out_shape=[jax.ShapeDtypeStruct((b, n, dk), BF16), jax.ShapeDtypeStruct((b, dk, n), BF16)],
        compiler_params=_cparams(("parallel",)),
        name="compress",
    )(k2, v2, pek, w1k, w2k, pev, w1v, w2v)


def _nsa_select_kernel(q_ref, kc_ref, vct_ref, misc_ref, ovt_ref, qaug_ref, oc_ref, gate_ref):
    tq = NSA_TQ
    rows = NSA_HEADS * tq
    jd = pl.program_id(1)
    step0 = jd * NSA_TK
    n_c = kc_ref.shape[0]
    log_g = int(math.log2(CMP_STRIDE))
    row_i = lax.broadcasted_iota(jnp.int32, (rows, LANES), 0) & (tq - 1)
    lane = lax.broadcasted_iota(jnp.int32, (rows, LANES), 1)
    lane_q = lax.broadcasted_iota(jnp.int32, (1, rows), 1) & (tq - 1)
    q_grp = jnp.where(lane == ((row_i + 1) >> log_g), 1.0, 0.0).astype(BF16)
    n_minus_g = (lax.broadcasted_iota(jnp.int32, (n_c, LANES), 0)
                 - lax.broadcasted_iota(jnp.int32, (n_c, LANES), 1))

    def spread_gates(r):
        n_gate = N_BRANCH * NSA_HEADS
        spread = jnp.where(lax.broadcasted_iota(jnp.int32, (LANES, n_gate * LANES), 0)
                           == (lax.broadcasted_iota(jnp.int32, (LANES, n_gate * LANES), 1) >> int(math.log2(LANES))),
                           1.0, 0.0).astype(BF16)
        yield
        g_hi, g_lo = _split(_sigmoid(misc_ref[r * tq:(r + 1) * tq, :]))
        yield
        gate_ref[r * tq:(r + 1) * tq, :] = _dot(g_hi, spread) + _dot(g_lo, spread)
        yield

    def heads_to_rows(q):
        return jnp.concatenate([q[:, h * NSA_HEAD_DIM:(h + 1) * NSA_HEAD_DIM] for h in range(NSA_HEADS)], axis=0)

    def prepare(r, n_vis):
        n_blk = n_vis * CMP_STRIDE // SLC_BLOCK
        q0 = step0 + r * tq
        qh = heads_to_rows(q_ref[r * tq:(r + 1) * tq, :])
        vis = jnp.where(n_minus_g[0:n_vis] <= (q0 >> log_g) - 2, 0.0, NEG).astype(BF16)
        st = _dot_nt(jnp.concatenate([kc_ref[0:n_vis, :], vis], axis=1), jnp.concatenate([qh, q_grp], axis=1))
        yield
        et = jnp.exp2(st - jnp.max(st, axis=0, keepdims=True))
        inv = jnp.where(q0 + lane_q >= CMP_BLOCK - 1, 1.0 / jnp.sum(et, axis=0, keepdims=True), 0.0)
        p_t = et * inv
        p_bf = p_t.astype(BF16)
        p_sum = p_t[:, 0:tq]
        for h in range(1, NSA_HEADS):
            p_sum = p_sum + p_t[:, h * tq:(h + 1) * tq]
        p_hi, p_lo = _split(p_sum)
        yield
        oc_ref[r] = _dot(vct_ref[:, 0:n_vis], p_bf).T
        ovt = ovt_ref[0:n_blk, 0:n_vis]
        imp = _dot(ovt, p_hi) + _dot(ovt, p_lo)
        yield
        blk = lax.broadcasted_iota(jnp.int32, (n_blk, tq), 0)
        cur = (q0 + lax.broadcasted_iota(jnp.int32, (n_blk, tq), 1)) >> int(math.log2(SLC_BLOCK))
        forced = (blk == 0) | (blk == cur) | (blk == cur - 1)
        val = jnp.where(forced | (blk > cur), -1.0, imp)
        blk_f = blk.astype(F32)
        for _ in range(SLC_TOPK - N_FORCED):
            top = jnp.max(val, axis=0, keepdims=True)
            first = jnp.min(jnp.where(val == top, blk_f, float(LANES)), axis=0, keepdims=True)
            val = jnp.where(blk_f == first, -2.0, val)
        bias_t = jnp.where(forced | (val == -2.0), 0.0, NEG)
        if n_blk < LANES:
            bias_t = jnp.concatenate([bias_t, jnp.full((LANES - n_blk, tq), NEG, F32)], axis=0)
        bias = bias_t.T.astype(BF16)
        qaug_ref[r] = jnp.concatenate([qh, jnp.concatenate([bias] * NSA_HEADS, axis=0)], axis=1)
        yield

    bucket_rows = min(CMP_BUCKET, n_c)
    steps_per_bucket = bucket_rows * CMP_STRIDE // NSA_TK
    for bucket in range(n_c // bucket_rows):
        @pl.when(jd // steps_per_bucket == bucket)
        def _(bucket=bucket):
            _trace_skewed([prepare(r, (bucket + 1) * bucket_rows) for r in range(NSA_SUB)]
                          + [spread_gates(r) for r in range(NSA_SUB)])


def _nsa_kernel(qaug_ref, oc_ref, gate_ref, ks_ref, vs_ref, kw_ref, vw_ref, og_ref, o_ref, m_ref, acc_ref):
    tq = NSA_TQ
    rows = NSA_HEADS * tq
    jd = pl.program_id(1)
    step0 = jd * NSA_TK
    row_i = lax.broadcasted_iota(jnp.int32, (rows, LANES), 0) & (tq - 1)
    lane = lax.broadcasted_iota(jnp.int32, (rows, LANES), 1)
    tri_diag = jnp.where(lane <= row_i, 0.0, NEG)
    tri_far = jnp.where(lane > row_i, 0.0, NEG)
    for r in range(NSA_SUB):
        m_ref[r] = jnp.full((rows, LANES), NEG, F32)
        acc_ref[r] = jnp.zeros((rows, 2 * NSA_HEAD_DIM), F32)

    def mask_last_chunk(sc):
        w = sc.shape[1]
        if w == LANES:
            return sc + tri_diag
        return jnp.concatenate([sc[:, :w - LANES], sc[:, w - LANES:] + tri_diag], axis=1)

    def with_ones(v):
        return jnp.concatenate([v, jnp.ones(v.shape, BF16)], axis=1)

    def flash(r, k0, w):
        sc = _dot_nt(qaug_ref[r], ks_ref[pl.ds(k0, w), :])
        yield
        m_i = m_ref[r]
        m_n = jnp.maximum(m_i, jnp.max(sc, axis=1, keepdims=True))
        alpha = jnp.exp2(m_i - m_n)
        p = jnp.exp2(sc - jnp.concatenate([m_n] * (w // LANES), axis=1)).astype(BF16)
        yield
        acc_ref[r] = (jnp.concatenate([alpha, alpha], axis=1) * acc_ref[r]
                      + _dot(p, with_ones(vs_ref[pl.ds(k0, w), :])))
        m_ref[r] = m_n
        yield

    def full_tiles(j0, n):
        spans = [(t, min(NSA_TKL // NSA_TK, n - t)) for t in range(0, n, NSA_TKL // NSA_TK)]
        _trace_skewed(flash(r, pl.multiple_of((j0 + t) * NSA_TK, NSA_TK), cnt * NSA_TK)
                      for t, cnt in spans for r in range(NSA_SUB))

    def tile_group(j, carry):
        full_tiles(j * NSA_UNROLL, NSA_UNROLL)
        return carry

    lax.fori_loop(0, jd // NSA_UNROLL, tile_group, 0)
    for rem in range(1, NSA_UNROLL):
        @pl.when(jd % NSA_UNROLL == rem)
        def _(rem=rem):
            full_tiles(jd - rem, rem)

    def finish(r, first_step):
        w = (r + 1) * tq
        d0 = pl.multiple_of(step0, NSA_TK)
        sc = mask_last_chunk(_dot_nt(qaug_ref[r], ks_ref[pl.ds(d0, w), :]))
        vt = vs_ref[pl.ds(d0, w), :]
        qh = qaug_ref[r][:, 0:NSA_HEAD_DIM]
        if first_step:
            sw = mask_last_chunk(_dot_nt(qh, kw_ref[0:w, :]))
            vw = vw_ref[0:w, :]
        else:
            w0 = pl.multiple_of(step0 + r * tq - WINDOW, tq)
            sw = _dot_nt(qh, kw_ref[pl.ds(w0, WIN_SLAB), :])
            sw = jnp.concatenate([sw[:, :LANES] + tri_far, sw[:, LANES:WIN_SLAB - LANES],
                                  sw[:, WIN_SLAB - LANES:] + tri_diag], axis=1)
            vw = vw_ref[pl.ds(w0, WIN_SLAB), :]
        yield
        m_i = m_ref[r]
        m_n = jnp.maximum(m_i, jnp.max(sc, axis=1, keepdims=True))
        alpha = jnp.exp2(m_i - m_n)
        p = jnp.exp2(sc - jnp.concatenate([m_n] * (w // LANES), axis=1)).astype(BF16)
        ew = jnp.exp2(sw - jnp.max(sw, axis=1, keepdims=True)).astype(BF16)
        yield
        acc = jnp.concatenate([alpha, alpha], axis=1) * acc_ref[r] + _dot(p, with_ones(vt))
        pv = _dot(ew, with_ones(vw))
        yield
        o_s = acc[:, 0:NSA_HEAD_DIM] * (1.0 / acc[:, NSA_HEAD_DIM:])
        o_w = pv[:, 0:NSA_HEAD_DIM] * (1.0 / pv[:, NSA_HEAD_DIM:])
        o_c = oc_ref[r]
        outs = []
        for h in range(NSA_HEADS):
            hr = slice(h * tq, (h + 1) * tq)
            gc, gs, gw = (gate_ref[r * tq:(r + 1) * tq,
                                   (N_BRANCH * h + k) * LANES:(N_BRANCH * h + k + 1) * LANES]
                          for k in range(N_BRANCH))
            o_h = gc * o_c[hr] + gs * o_s[hr] + gw * o_w[hr]
            outs.append(_rms(o_h, og_ref[:, h * NSA_HEAD_DIM:(h + 1) * NSA_HEAD_DIM]))
        o_ref[r * tq:(r + 1) * tq, :] = jnp.concatenate(outs, axis=1).astype(BF16)
        yield

    @pl.when(jd == 0)
    def _():
        _trace_skewed(finish(r, True) for r in range(NSA_SUB))

    @pl.when(jd > 0)
    def _():
        _trace_skewed(finish(r, False) for r in range(NSA_SUB))


def _nsa(nq, kc, vct, ks, vs, kw, vw, misc, ovt, og):
    b, s, wq = nq.shape
    ts = NSA_SUB * NSA_TQ
    rows = NSA_HEADS * NSA_TQ

    def per_batch(a):
        return pl.BlockSpec((None,) + a.shape[1:], lambda i, j: (i, 0, 0))

    def per_tile(a):
        return pl.BlockSpec((None, ts, a.shape[2]), lambda i, j: (i, j, 0))

    def full(a):
        return pl.BlockSpec(a.shape, lambda i, j: (0, 0))

    n_sub = s // NSA_TQ
    gate_w = N_BRANCH * NSA_HEADS * LANES

    def per_step(width):
        return pl.BlockSpec((None, NSA_SUB, rows, width), lambda i, j: (i, j, 0, 0))

    qaug, o_cmp, gates = pl.pallas_call(
        _nsa_select_kernel,
        grid=(b, s // ts),
        in_specs=[per_tile(nq), per_batch(kc), per_batch(vct), per_tile(misc), full(ovt)],
        out_specs=[per_step(2 * NSA_HEAD_DIM), per_step(NSA_HEAD_DIM),
                   pl.BlockSpec((None, ts, gate_w), lambda i, j: (i, j, 0))],
        out_shape=[jax.ShapeDtypeStruct((b, n_sub, rows, 2 * NSA_HEAD_DIM), BF16),
                   jax.ShapeDtypeStruct((b, n_sub, rows, NSA_HEAD_DIM), F32),
                   jax.ShapeDtypeStruct((b, s, gate_w), F32)],
        compiler_params=_cparams(("parallel", "arbitrary")),
        name="nsa_select",
    )(nq, kc, vct, misc, ovt)
    return pl.pallas_call(
        _nsa_kernel,
        grid=(b, s // ts),
        in_specs=[per_step(2 * NSA_HEAD_DIM), per_step(NSA_HEAD_DIM),
                  pl.BlockSpec((None, ts, gate_w), lambda i, j: (i, j, 0)),
                  per_batch(ks), per_batch(vs), per_batch(kw), per_batch(vw), full(og)],
        out_specs=pl.BlockSpec((None, ts, wq), lambda i, j: (i, j, 0)),
        out_shape=jax.ShapeDtypeStruct((b, s, wq), BF16),
        scratch_shapes=[pltpu.VMEM((NSA_SUB, rows, LANES), F32),
                        pltpu.VMEM((NSA_SUB, rows, 2 * NSA_HEAD_DIM), F32)],
        compiler_params=_cparams(("parallel", "arbitrary")),
        name="nsa",
    )(qaug, o_cmp, gates, ks, vs, kw, vw, og)


def _gla_kernel(q_ref, k_ref, v_ref, og_ref, misc_ref, up_ref, bias_ref, g_ref, o_ref, st_ref):
    c = GLA_CHUNK
    kw = GLA_HEADS * GLA_KEY_DIM
    tg = q_ref.shape[1]
    n_chunk = tg // c
    log_c = int(math.log2(c))

    @pl.when(pl.program_id(1) == 0)
    def _():
        st_ref[...] = jnp.zeros_like(st_ref)

    ri = lax.broadcasted_iota(jnp.int32, (tg, tg), 0)
    ci = lax.broadcasted_iota(jnp.int32, (tg, tg), 1)
    low = ((ri >> log_c) == (ci >> log_c)) & (ci <= ri)
    causal = jnp.where(low, 1.0, 0.0)
    tri = causal.astype(BF16)
    lane_head = lax.broadcasted_iota(jnp.int32, (1, kw), 1) >> int(math.log2(GLA_KEY_DIM))
    head_mask = [lane_head == h for h in range(GLA_HEADS)]
    u_hi, u_lo = _split(up_ref[...])

    def stack_heads(x):
        return jnp.concatenate([jnp.where(head_mask[h], x, 0.0) for h in range(GLA_HEADS)], axis=0)

    def sequence(bb):
        m_hi, m_lo = _split(misc_ref[bb])
        g_pre = _dot(m_hi, u_hi) + _dot(m_lo, u_hi) + _dot(m_hi, u_lo) + bias_ref[...]
        yield
        g_log = (jnp.minimum(g_pre, 0.0) - jnp.log(1.0 + jnp.exp(-jnp.abs(g_pre)))) * (1.0 / GLA_GATE_NORM)
        g_hi, g_lo = _split(g_log)
        yield
        bcum = _dot(tri, g_hi) + _dot(tri, g_lo)
        yield
        b_last = jnp.concatenate([jnp.broadcast_to(bcum[(cc + 1) * c - 1:(cc + 1) * c, :], (c, kw))
                                  for cc in range(n_chunk)], axis=0)
        qf = q_ref[bb].astype(F32)
        kf = k_ref[bb].astype(F32)
        v = v_ref[bb]
        q_dec = qf * jnp.exp(bcum) * (GLA_KEY_DIM ** -0.5)
        k_dec = (kf * jnp.exp(-bcum)).astype(BF16)
        k_end = kf * jnp.exp(b_last - bcum)
        q_heads = [jnp.where(head_mask[h], q_dec, 0.0).astype(BF16) for h in range(GLA_HEADS)]
        yield
        scores = [_dot_nt(q_heads[h], k_dec) for h in range(GLA_HEADS)]
        yield
        a = [(scores[h] * causal).astype(BF16) for h in range(GLA_HEADS)]
        yield
        intra = [_dot(a[h], v[:, h * GLA_VAL_DIM:(h + 1) * GLA_VAL_DIM]) for h in range(GLA_HEADS)]
        st = st_ref[bb]
        inter = []
        for cc in range(n_chunk):
            r = slice(cc * c, (cc + 1) * c)
            inter.append(_dot_nt(stack_heads(q_dec[r]).astype(BF16), st.astype(BF16)))
            km = stack_heads(k_end[r]).astype(BF16)
            vst = jnp.concatenate([v[r, h * GLA_VAL_DIM:(h + 1) * GLA_VAL_DIM] for h in range(GLA_HEADS)],
                                  axis=0)
            st = st * jnp.exp(b_last[cc * c:cc * c + 1, :]) + _dot_tn(vst, km)
        st_ref[bb] = st
        yield
        for h in range(GLA_HEADS):
            vs = slice(h * GLA_VAL_DIM, (h + 1) * GLA_VAL_DIM)
            inter_h = jnp.concatenate([inter[cc][h * c:(h + 1) * c] for cc in range(n_chunk)], axis=0)
            gate = og_ref[bb, :, vs].astype(F32)
            o_ref[bb, :, vs] = (_rms(intra[h] + inter_h, g_ref[:, vs]) * (gate * _sigmoid(gate))).astype(BF16)
        yield

    _trace_skewed(sequence(bb) for bb in range(q_ref.shape[0]))


def _gla(gq, gk, gv, gog, misc, up_pad, bias, g):
    b, s, _ = gq.shape
    tg = min(GLA_TG, s)
    nb = GLA_NB if b % GLA_NB == 0 else 1

    def per_tile(a):
        return pl.BlockSpec((nb, tg, a.shape[2]), lambda i, j: (i, j, 0))

    def full(a):
        return pl.BlockSpec(a.shape, lambda i, j: (0, 0))

    return pl.pallas_call(
        _gla_kernel,
        grid=(b // nb, s // tg),
        in_specs=[per_tile(gq), per_tile(gk), per_tile(gv), per_tile(gog), per_tile(misc),
                  full(up_pad), full(bias), full(g)],
        out_specs=per_tile(gv),
        out_shape=jax.ShapeDtypeStruct(gv.shape, BF16),
        scratch_shapes=[pltpu.VMEM((nb, GLA_VAL_DIM, GLA_HEADS * GLA_KEY_DIM), F32)],
        compiler_params=_cparams(("parallel", "arbitrary")),
        name="gla",
    )(gq, gk, gv, gog, misc, up_pad, bias, g)


def _overlap_matrix(n_rows):
    n = np.arange(n_rows)[:, None] * CMP_STRIDE
    s = np.arange(LANES)[None, :] * SLC_BLOCK
    return ((n < s + SLC_BLOCK) & (n + CMP_BLOCK > s)).astype(np.float32)


def _layer(h, p, seq, batch, ffn1_pre_g, ffn1_post_g, ffn1_w_gate, ffn1_w_up, ffn1_w_down, mix_pre_g, mix_post_g,
           w_in, cmp_k_pe, cmp_k_w1, cmp_k_w2, cmp_v_pe, cmp_v_w1, cmp_v_w2, nsa_out_g, gla_gate_up, gla_gate_bias,
           gla_out_g, w_out, ffn2_pre_g, ffn2_post_g, ffn2_w_gate, ffn2_w_up, ffn2_w_down, ple_proj, ple_gate,
           ple_post_g):
    d = h.shape[1]
    row = lambda a: a.reshape(1, -1).astype(F32)
    bf = lambda a: a.astype(BF16)

    h1 = _ffn(h, row(ffn1_pre_g), row(ffn1_post_g), bf(ffn1_w_gate), bf(ffn1_w_up), bf(ffn1_w_down))

    o_gate = 512 + 6 * 128
    o_gq = o_gate + 3 * NSA_HEADS
    o_gk = o_gq + GLA_HEADS * GLA_KEY_DIM
    o_gv = o_gk + GLA_HEADS * GLA_KEY_DIM
    o_gd = o_gv + GLA_HEADS * GLA_VAL_DIM
    o_og = o_gd + GLA_GATE_RANK
    pad = jnp.zeros((d, LANES - 3 * NSA_HEADS - GLA_GATE_RANK), w_in.dtype)
    w_all = bf(jnp.concatenate([w_in[:, :o_gate], w_in[:, o_gate:o_gq], w_in[:, o_gd:o_og], pad,
                                w_in[:, o_gq:o_gd], w_in[:, o_og:]], axis=1))
    nq, kc_in, vc_in, ks, vs, kw, vw, misc, gq, gk, gv, gog = _proj(h1, row(mix_pre_g), w_all, seq)

    n_grp = seq // CMP_STRIDE
    grp_w = CMP_STRIDE * NSA_HEAD_DIM
    kc, vc = _compress(
        kc_in.reshape(batch, n_grp, grp_w), vc_in.reshape(batch, n_grp, grp_w),
        cmp_k_pe.reshape(2, grp_w).astype(F32), bf(cmp_k_w1.reshape(2 * grp_w, -1)), bf(cmp_k_w2),
        cmp_v_pe.reshape(2, grp_w).astype(F32), bf(cmp_v_w1.reshape(2 * grp_w, -1)), bf(cmp_v_w2))

    b3 = lambda a: a.reshape(batch, seq, a.shape[-1])
    ovt = jnp.asarray(_overlap_matrix(n_grp).T, BF16)
    o_nsa = _nsa(b3(nq), kc, vc, b3(ks), b3(vs), b3(kw), b3(vw), b3(misc), ovt, row(nsa_out_g))

    up_pad = jnp.zeros((LANES, GLA_HEADS * GLA_KEY_DIM), F32)
    up_pad = up_pad.at[_MISC_GDOWN:_MISC_GDOWN + GLA_GATE_RANK].set(gla_gate_up.astype(F32))
    o_gla = _gla(b3(gq), b3(gk), b3(gv), b3(gog), b3(misc), up_pad, row(gla_gate_bias), row(gla_out_g))

    w_o = bf(w_out)
    half = NSA_HEADS * NSA_HEAD_DIM
    return _ffn(h1, row(ffn2_pre_g), row(ffn2_post_g), bf(ffn2_w_gate), bf(ffn2_w_up), bf(ffn2_w_down),
                mix=(o_nsa.reshape(-1, half), o_gla.reshape(-1, half), w_o[:half], w_o[half:], row(mix_post_g)),
                ple=(p, bf(ple_proj), bf(ple_gate), row(ple_post_g)))


def kernel(x, p, ffn1_pre_g, ffn1_post_g, ffn1_w_gate, ffn1_w_up, ffn1_w_down, mix_pre_g, mix_post_g, w_in, cmp_k_pe, cmp_k_w1, cmp_k_w2, cmp_v_pe, cmp_v_w1, cmp_v_w2, nsa_out_g, gla_gate_up, gla_gate_bias, gla_out_g, w_out, ffn2_pre_g, ffn2_post_g, ffn2_w_gate, ffn2_w_up, ffn2_w_down, ple_proj, ple_gate, ple_post_g):
    batch, seq, d = x.shape
    depth = p.shape[0]
    assert seq % NSA_TK == 0 and seq >= WIN_SLAB and seq // SLC_BLOCK <= LANES
    h = x.reshape(batch * seq, d)
    params = (ffn1_pre_g, ffn1_post_g, ffn1_w_gate, ffn1_w_up, ffn1_w_down, mix_pre_g, mix_post_g, w_in, cmp_k_pe,
              cmp_k_w1, cmp_k_w2, cmp_v_pe, cmp_v_w1, cmp_v_w2, nsa_out_g, gla_gate_up, gla_gate_bias, gla_out_g,
              w_out, ffn2_pre_g, ffn2_post_g, ffn2_w_gate, ffn2_w_up, ffn2_w_down, ple_proj, ple_gate, ple_post_g)
    for i in range(depth):
        h = _layer(h, p[i].reshape(batch * seq, -1), seq, batch, *[a[i] for a in params])
    return h.reshape(batch, seq, d)
```

```python
import functools
import math

import numpy as np
import jax
import jax.numpy as jnp
from jax import lax
from jax.experimental import pallas as pl
from jax.experimental.pallas import tpu as pltpu

F32 = jnp.float32
BF16 = jnp.bfloat16

EPS = 1e-6
NSA_HEADS = 4
NSA_HEAD_DIM = 128
NSA_KV_W = 128
CMP_BLOCK = 32
CMP_STRIDE = 16
SLC_BLOCK = 64
SLC_TOPK = 16
WINDOW = 512
FORCE_BONUS = 1e4
NEG = -1e30
GLA_HEADS = 4
GLA_KEY_DIM = 64
GLA_VAL_DIM = 128
GLA_GATE_RANK = 16
GLA_GATE_NORM = 16.0
GLA_CHUNK = 64

LOG2E = 1.4426950408889634
N_FORCED = 3
N_BRANCH = 3

LANES = 128
VMEM_LIMIT = 56 * 1024 * 1024

FFN_TM = 1024
FFN_TF = 256
TOK_TM = 1024
NSA_TQ = 128
NSA_SUB = 4
NSA_TK = 512
NSA_UNROLL = 4
NSA_TKL = 2048
CMP_BUCKET = 128
GLA_TG = 256
GLA_NB = 8
WIN_SLAB = WINDOW + NSA_TQ


def _dot(a, b):
    return jnp.dot(a, b, preferred_element_type=F32)


def _dot_nt(a, b):
    return lax.dot_general(a, b, (((1,), (1,)), ((), ())), preferred_element_type=F32)


def _dot_tn(a, b):
    return lax.dot_general(a, b, (((0,), (0,)), ((), ())), preferred_element_type=F32)


def _split(x):
    hi = x.astype(BF16)
    lo = (x - hi.astype(F32)).astype(BF16)
    return hi, lo


def _rms(x, g):
    return x * lax.rsqrt(jnp.mean(x * x, axis=-1, keepdims=True) + EPS) * g


def _sigmoid(x):
    return 1.0 / (1.0 + jnp.exp(-x))


def _cparams(sem):
    return pltpu.CompilerParams(dimension_semantics=sem, vmem_limit_bytes=VMEM_LIMIT)


def _trace_skewed(stage_generators):
    pending = list(stage_generators)
    running = []
    while running or pending:
        if pending:
            running.append(pending.pop(0))
        for gen in list(running):
            if next(gen, "done") == "done":
                running.remove(gen)


def _ffn_kernel(n_mix, n_ple, *refs):
    h_ref, pre_ref, post_ref, wg_ref, wu_ref, wd_ref = refs[:6]
    mix_refs = refs[6:6 + n_mix]
    ple_refs = refs[6 + n_mix:6 + n_mix + n_ple]
    o_ref = refs[-1]
    h = h_ref[...]
    if n_mix:
        on_ref, ogl_ref, wa_ref, wb_ref, mg_ref = mix_refs
        h = h + _rms(_dot(on_ref[...], wa_ref[...]) + _dot(ogl_ref[...], wb_ref[...]), mg_ref[...])
    u = _rms(h, pre_ref[...]).astype(BF16)
    acc = None
    for c in range(wg_ref.shape[1] // FFN_TF):
        sl = slice(c * FFN_TF, (c + 1) * FFN_TF)
        a = _dot(u, wg_ref[:, sl])
        b = _dot(u, wu_ref[:, sl])
        part = _dot((a * _sigmoid(a) * b).astype(BF16), wd_ref[sl, :])
        acc = part if acc is None else acc + part
    h = h + 0.5 * _rms(acc, post_ref[...])
    if n_ple:
        p_ref, wp_ref, wgate_ref, pg_ref = ple_refs
        e = _dot(p_ref[...].astype(BF16), wp_ref[...])
        h = h + _rms(e * _sigmoid(_dot(h.astype(BF16), wgate_ref[...])), pg_ref[...])
    o_ref[...] = h


def _resident(a):
    return pl.BlockSpec(a.shape, lambda *_: (0,) * a.ndim, pipeline_mode=pl.Buffered(1))


def _ffn(h, pre_g, post_g, wg, wu, wd, mix=(), ple=()):
    t, d = h.shape
    tm = min(FFN_TM, t)

    def tile(a):
        return pl.BlockSpec((tm, a.shape[1]), lambda i: (i, 0))

    args = [h, pre_g, post_g, wg, wu, wd, *mix, *ple]
    specs = [tile(h)] + [_resident(a) for a in (pre_g, post_g, wg, wu, wd)]
    if mix:
        specs += [tile(mix[0]), tile(mix[1])] + [_resident(a) for a in mix[2:]]
    if ple:
        specs += [tile(ple[0])] + [_resident(a) for a in ple[1:]]
    return pl.pallas_call(
        functools.partial(_ffn_kernel, len(mix), len(ple)),
        grid=(t // tm,),
        in_specs=specs,
        out_specs=tile(h),
        out_shape=jax.ShapeDtypeStruct((t, d), F32),
        compiler_params=pltpu.CompilerParams(dimension_semantics=("parallel",), vmem_limit_bytes=VMEM_LIMIT,
                                             allow_input_fusion=[a.dtype == BF16 and a.ndim == 2 and a.shape[0] > 1
                                                                 and a.shape[0] != t for a in args]),
        name="ffn",
    )(*args)


_P_NQ = (0, 512)
_P_KC = (512, 640)
_P_VC = (640, 768)
_P_KS = (768, 896)
_P_VS = (896, 1024)
_P_KW = (1024, 1152)
_P_VW = (1152, 1280)
_P_MISC = (1280, 1408)
_P_GQ = (1408, 1664)
_P_GK = (1664, 1920)
_P_GV = (1920, 2432)
_P_GOG = (2432, 2944)
_P_WIDTH = 2944
_MISC_GDOWN = 3 * NSA_HEADS


def _proj_kernel(seq, h_ref, g_ref, w_ref, nq_ref, kc_ref, vc_ref, ks_ref, vs_ref, kw_ref, vw_ref,
                 misc_ref, gq_ref, gk_ref, gv_ref, gog_ref, kc_tmp, vc_tmp):
    tm = h_ref.shape[0]
    u = _rms(h_ref[...], g_ref[...]).astype(BF16)

    def grp(lohi):
        return _dot(u, w_ref[:, lohi[0]:lohi[1]])

    nq_ref[...] = (grp(_P_NQ) * (NSA_HEAD_DIM ** -0.5 * LOG2E)).astype(BF16)
    kv_c = grp((_P_KC[0], _P_VC[1]))
    kc_tmp[...] = kv_c[:, 0:LANES]
    vc_tmp[...] = kv_c[:, LANES:2 * LANES]
    for l in range(CMP_STRIDE):
        rows_l = pl.ds(l, tm // CMP_STRIDE, stride=CMP_STRIDE)
        kc_ref[:, l * LANES:(l + 1) * LANES] = kc_tmp[rows_l, :].astype(BF16)
        vc_ref[:, l * LANES:(l + 1) * LANES] = vc_tmp[rows_l, :].astype(BF16)
    base = (pl.program_id(0) % (seq // tm)) * tm
    row = lax.broadcasted_iota(jnp.int32, (tm, LANES), 0)
    lane = lax.broadcasted_iota(jnp.int32, (tm, LANES), 1)
    onehot = jnp.where(lane == ((base + row) >> int(math.log2(SLC_BLOCK))), 1.0, 0.0).astype(BF16)
    kv_s = grp((_P_KS[0], _P_VS[1])).astype(BF16)
    ks_ref[:, 0:LANES] = kv_s[:, 0:LANES]
    ks_ref[:, LANES:2 * LANES] = onehot
    vs_ref[...] = kv_s[:, LANES:2 * LANES]
    kv_w = grp((_P_KW[0], _P_VW[1])).astype(BF16)
    kw_ref[...] = kv_w[:, 0:LANES]
    vw_ref[...] = kv_w[:, LANES:2 * LANES]
    misc_ref[...] = grp(_P_MISC)
    gq_ref[...] = grp(_P_GQ).astype(BF16)
    gk_ref[...] = grp(_P_GK).astype(BF16)
    gv_ref[...] = grp(_P_GV).astype(BF16)
    gog_ref[...] = grp(_P_GOG).astype(BF16)


def _proj(h, g, w_all, seq):
    t, d = h.shape
    tm = min(TOK_TM, seq)
    grp_w = CMP_STRIDE * NSA_KV_W
    outs = [(tm, 512, BF16), (tm // CMP_STRIDE, grp_w, BF16), (tm // CMP_STRIDE, grp_w, BF16), (tm, 256, BF16),
            (tm, 128, BF16), (tm, 128, BF16), (tm, 128, BF16), (tm, 128, F32), (tm, 256, BF16), (tm, 256, BF16),
            (tm, 512, BF16), (tm, 512, BF16)]
    return pl.pallas_call(
        functools.partial(_proj_kernel, seq),
        grid=(t // tm,),
        in_specs=[
            pl.BlockSpec((tm, d), lambda i: (i, 0)),
            pl.BlockSpec((1, d), lambda i: (0, 0)),
            pl.BlockSpec((d, _P_WIDTH), lambda i: (0, 0)),
        ],
        out_specs=[pl.BlockSpec((r, w), lambda i: (i, 0)) for r, w, _ in outs],
        out_shape=[jax.ShapeDtypeStruct((t // tm * r, w), dt) for r, w, dt in outs],
        scratch_shapes=[pltpu.VMEM((tm, NSA_KV_W), F32), pltpu.VMEM((tm, NSA_KV_W), F32)],
        compiler_params=_cparams(("parallel",)),
        name="proj",
    )(h, g, w_all)


def _compress_kernel(k_ref, v_ref, pek_ref, w1k_ref, w2k_ref, pev_ref, w1v_ref, w2v_ref, kc_ref, vc_ref):
    half = CMP_STRIDE * NSA_HEAD_DIM

    def one(x_ref, pe_ref, w1_ref, w2_ref):
        x = x_ref[...].astype(F32)
        n = x.shape[0]
        a = _dot((x + pe_ref[0:1, :]).astype(BF16), w1_ref[0:half, :])
        b = _dot((x + pe_ref[1:2, :]).astype(BF16), w1_ref[half:2 * half, :])
        hid = a + pltpu.roll(b, n - 1, 0)
        act = (hid * _sigmoid(hid)).astype(BF16)
        return _dot(act, w2_ref[...])

    kc_ref[...] = one(k_ref, pek_ref, w1k_ref, w2k_ref).astype(BF16)
    vc_ref[...] = one(v_ref, pev_ref, w1v_ref, w2v_ref).T.astype(BF16)


def _compress(k2, v2, pek, w1k, w2k, pev, w1v, w2v):
    b, n, w = k2.shape
    dk = w2k.shape[1]
    big = pl.BlockSpec((None, n, w), lambda i: (i, 0, 0))

    def full(a):
        return pl.BlockSpec(a.shape, lambda i: (0,) * a.ndim)

    return pl.pallas_call(
        _compress_kernel,
        grid=(b,),
        in_specs=[big, big, full(pek), full(w1k), full(w2k), full(pev), full(w1v), full(w2v)],
        out_specs=[pl.BlockSpec((None, n, dk), lambda i: (i, 0, 0)), pl.BlockSpec((None, dk, n), lambda i: (i, 0, 0))],
        out_shape=[jax.ShapeDtypeStruct((b, n, dk), BF16), jax.ShapeDtypeStruct((b, dk, n), BF16)],
        compiler_params=_cparams(("parallel",)),
        name="compress",
    )(k2, v2, pek, w1k, w2k, pev, w1v, w2v)


def _nsa_kernel(q_ref, kc_ref, vct_ref, ks_ref, vs_ref, kw_ref, vw_ref, misc_ref, ovt_ref, og_ref, o_ref,
                qaug_ref, m_ref, acc_ref, oc_ref, gate_ref):
    tq = NSA_TQ
    rows = NSA_HEADS * tq
    jd = pl.program_id(1)
    step0 = jd * NSA_TK
    n_c = kc_ref.shape[0]
    log_g = int(math.log2(CMP_STRIDE))
    row_i = lax.broadcasted_iota(jnp.int32, (rows, LANES), 0) & (tq - 1)
    lane = lax.broadcasted_iota(jnp.int32, (rows, LANES), 1)
    lane_q = lax.broadcasted_iota(jnp.int32, (1, rows), 1) & (tq - 1)
    tri_diag = jnp.where(lane <= row_i, 0.0, NEG)
    tri_far = jnp.where(lane > row_i, 0.0, NEG)
    q_grp = jnp.where(lane == ((row_i + 1) >> log_g), 1.0, 0.0).astype(BF16)
    n_minus_g = (lax.broadcasted_iota(jnp.int32, (n_c, LANES), 0)
                 - lax.broadcasted_iota(jnp.int32, (n_c, LANES), 1))

    def spread_gates(r):
        n_gate = N_BRANCH * NSA_HEADS
        spread = jnp.where(lax.broadcasted_iota(jnp.int32, (LANES, n_gate * LANES), 0)
                           == (lax.broadcasted_iota(jnp.int32, (LANES, n_gate * LANES), 1) >> int(math.log2(LANES))),
                           1.0, 0.0).astype(BF16)
        yield
        g_hi, g_lo = _split(_sigmoid(misc_ref[r * tq:(r + 1) * tq, :]))
        yield
        gate_ref[r * tq:(r + 1) * tq, :] = _dot(g_hi, spread) + _dot(g_lo, spread)
        yield

    def heads_to_rows(q):
        return jnp.concatenate([q[:, h * NSA_HEAD_DIM:(h + 1) * NSA_HEAD_DIM] for h in range(NSA_HEADS)], axis=0)

    def prepare(r, n_vis):
        n_blk = n_vis * CMP_STRIDE // SLC_BLOCK
        q0 = step0 + r * tq
        qh = heads_to_rows(q_ref[r * tq:(r + 1) * tq, :])
        vis = jnp.where(n_minus_g[0:n_vis] <= (q0 >> log_g) - 2, 0.0, NEG).astype(BF16)
        st = _dot_nt(jnp.concatenate([kc_ref[0:n_vis, :], vis], axis=1), jnp.concatenate([qh, q_grp], axis=1))
        yield
        et = jnp.exp2(st - jnp.max(st, axis=0, keepdims=True))
        inv = jnp.where(q0 + lane_q >= CMP_BLOCK - 1, 1.0 / jnp.sum(et, axis=0, keepdims=True), 0.0)
        p_t = et * inv
        p_bf = p_t.astype(BF16)
        p_sum = p_t[:, 0:tq]
        for h in range(1, NSA_HEADS):
            p_sum = p_sum + p_t[:, h * tq:(h + 1) * tq]
        p_hi, p_lo = _split(p_sum)
        yield
        oc_ref[r] = _dot(vct_ref[:, 0:n_vis], p_bf).T
        ovt = ovt_ref[0:n_blk, 0:n_vis]
        imp = _dot(ovt, p_hi) + _dot(ovt, p_lo)
        yield
        blk = lax.broadcasted_iota(jnp.int32, (n_blk, tq), 0)
        cur = (q0 + lax.broadcasted_iota(jnp.int32, (n_blk, tq), 1)) >> int(math.log2(SLC_BLOCK))
        forced = (blk == 0) | (blk == cur) | (blk == cur - 1)
        val = jnp.where(forced | (blk > cur), -1.0, imp)
        blk_f = blk.astype(F32)
        for _ in range(SLC_TOPK - N_FORCED):
            top = jnp.max(val, axis=0, keepdims=True)
            first = jnp.min(jnp.where(val == top, blk_f, float(LANES)), axis=0, keepdims=True)
            val = jnp.where(blk_f == first, -2.0, val)
        bias_t = jnp.where(forced | (val == -2.0), 0.0, NEG)
        if n_blk < LANES:
            bias_t = jnp.concatenate([bias_t, jnp.full((LANES - n_blk, tq), NEG, F32)], axis=0)
        bias = bias_t.T.astype(BF16)
        qaug_ref[r] = jnp.concatenate([qh, jnp.concatenate([bias] * NSA_HEADS, axis=0)], axis=1)
        m_ref[r] = jnp.full((rows, LANES), NEG, F32)
        acc_ref[r] = jnp.zeros((rows, 2 * NSA_HEAD_DIM), F32)
        yield

    bucket_rows = min(CMP_BUCKET, n_c)
    steps_per_bucket = bucket_rows * CMP_STRIDE // NSA_TK
    for bucket in range(n_c // bucket_rows):
        @pl.when(jd // steps_per_bucket == bucket)
        def _(bucket=bucket):
            _trace_skewed([prepare(r, (bucket + 1) * bucket_rows) for r in range(NSA_SUB)]
                          + [spread_gates(r) for r in range(NSA_SUB)])

    def mask_last_chunk(sc):
        w = sc.shape[1]
        if w == LANES:
            return sc + tri_diag
        return jnp.concatenate([sc[:, :w - LANES], sc[:, w - LANES:] + tri_diag], axis=1)

    def with_ones(v):
        return jnp.concatenate([v, jnp.ones(v.shape, BF16)], axis=1)

    def flash(r, k0, w):
        sc = _dot_nt(qaug_ref[r], ks_ref[pl.ds(k0, w), :])
        yield
        m_i = m_ref[r]
        m_n = jnp.maximum(m_i, jnp.max(sc, axis=1, keepdims=True))
        alpha = jnp.exp2(m_i - m_n)
        p = jnp.exp2(sc - jnp.concatenate([m_n] * (w // LANES), axis=1)).astype(BF16)
        yield
        acc_ref[r] = (jnp.concatenate([alpha, alpha], axis=1) * acc_ref[r]
                      + _dot(p, with_ones(vs_ref[pl.ds(k0, w), :])))
        m_ref[r] = m_n
        yield

    def full_tiles(j0, n):
        spans = [(t, min(NSA_TKL // NSA_TK, n - t)) for t in range(0, n, NSA_TKL // NSA_TK)]
        _trace_skewed(flash(r, pl.multiple_of((j0 + t) * NSA_TK, NSA_TK), cnt * NSA_TK)
                      for t, cnt in spans for r in range(NSA_SUB))

    def tile_group(j, carry):
        full_tiles(j * NSA_UNROLL, NSA_UNROLL)
        return carry

    lax.fori_loop(0, jd // NSA_UNROLL, tile_group, 0)
    for rem in range(1, NSA_UNROLL):
        @pl.when(jd % NSA_UNROLL == rem)
        def _(rem=rem):
            full_tiles(jd - rem, rem)

    def finish(r, first_step):
        w = (r + 1) * tq
        d0 = pl.multiple_of(step0, NSA_TK)
        sc = mask_last_chunk(_dot_nt(qaug_ref[r], ks_ref[pl.ds(d0, w), :]))
        vt = vs_ref[pl.ds(d0, w), :]
        qh = qaug_ref[r][:, 0:NSA_HEAD_DIM]
        if first_step:
            sw = mask_last_chunk(_dot_nt(qh, kw_ref[0:w, :]))
            vw = vw_ref[0:w, :]
        else:
            w0 = pl.multiple_of(step0 + r * tq - WINDOW, tq)
            sw = _dot_nt(qh, kw_ref[pl.ds(w0, WIN_SLAB), :])
            sw = jnp.concatenate([sw[:, :LANES] + tri_far, sw[:, LANES:WIN_SLAB - LANES],
                                  sw[:, WIN_SLAB - LANES:] + tri_diag], axis=1)
            vw = vw_ref[pl.ds(w0, WIN_SLAB), :]
        yield
        m_i = m_ref[r]
        m_n = jnp.maximum(m_i, jnp.max(sc, axis=1, keepdims=True))
        alpha = jnp.exp2(m_i - m_n)
        p = jnp.exp2(sc - jnp.concatenate([m_n] * (w // LANES), axis=1)).astype(BF16)
        ew = jnp.exp2(sw - jnp.max(sw, axis=1, keepdims=True)).astype(BF16)
        yield
        acc = jnp.concatenate([alpha, alpha], axis=1) * acc_ref[r] + _dot(p, with_ones(vt))
        pv = _dot(ew, with_ones(vw))
        yield
        o_s = acc[:, 0:NSA_HEAD_DIM] * (1.0 / acc[:, NSA_HEAD_DIM:])
        o_w = pv[:, 0:NSA_HEAD_DIM] * (1.0 / pv[:, NSA_HEAD_DIM:])
        o_c = oc_ref[r]
        outs = []
        for h in range(NSA_HEADS):
            hr = slice(h * tq, (h + 1) * tq)
            gc, gs, gw = (gate_ref[r * tq:(r + 1) * tq,
                                   (N_BRANCH * h + k) * LANES:(N_BRANCH * h + k + 1) * LANES]
                          for k in range(N_BRANCH))
            o_h = gc * o_c[hr] + gs * o_s[hr] + gw * o_w[hr]
            outs.append(_rms(o_h, og_ref[:, h * NSA_HEAD_DIM:(h + 1) * NSA_HEAD_DIM]))
        o_ref[r * tq:(r + 1) * tq, :] = jnp.concatenate(outs, axis=1).astype(BF16)
        yield

    @pl.when(jd == 0)
    def _():
        _trace_skewed(finish(r, True) for r in range(NSA_SUB))

    @pl.when(jd > 0)
    def _():
        _trace_skewed(finish(r, False) for r in range(NSA_SUB))


def _nsa(nq, kc, vct, ks, vs, kw, vw, misc, ovt, og):
    b, s, wq = nq.shape
    ts = NSA_SUB * NSA_TQ
    rows = NSA_HEADS * NSA_TQ

    def per_batch(a):
        return pl.BlockSpec((None,) + a.shape[1:], lambda i, j: (i, 0, 0))

    def per_tile(a):
        return pl.BlockSpec((None, ts, a.shape[2]), lambda i, j: (i, j, 0))

    def full(a):
        return pl.BlockSpec(a.shape, lambda i, j: (0, 0))

    return pl.pallas_call(
        _nsa_kernel,
        grid=(b, s // ts),
        in_specs=[per_tile(nq), per_batch(kc), per_batch(vct), per_batch(ks), per_batch(vs), per_batch(kw),
                  per_batch(vw), per_tile(misc), full(ovt), full(og)],
        out_specs=pl.BlockSpec((None, ts, wq), lambda i, j: (i, j, 0)),
        out_shape=jax.ShapeDtypeStruct((b, s, wq), BF16),
        scratch_shapes=[pltpu.VMEM((NSA_SUB, rows, 2 * NSA_HEAD_DIM), BF16),
                        pltpu.VMEM((NSA_SUB, rows, LANES), F32),
                        pltpu.VMEM((NSA_SUB, rows, 2 * NSA_HEAD_DIM), F32),
                        pltpu.VMEM((NSA_SUB, rows, NSA_HEAD_DIM), F32),
                        pltpu.VMEM((ts, N_BRANCH * NSA_HEADS * LANES), F32)],
        compiler_params=_cparams(("parallel", "arbitrary")),
        name="nsa",
    )(nq, kc, vct, ks, vs, kw, vw, misc, ovt, og)


def _gla_kernel(q_ref, k_ref, v_ref, og_ref, misc_ref, up_ref, bias_ref, g_ref, o_ref, st_ref):
    c = GLA_CHUNK
    kw = GLA_HEADS * GLA_KEY_DIM
    tg = q_ref.shape[1]
    n_chunk = tg // c
    log_c = int(math.log2(c))

    @pl.when(pl.program_id(1) == 0)
    def _():
        st_ref[...] = jnp.zeros_like(st_ref)

    ri = lax.broadcasted_iota(jnp.int32, (tg, tg), 0)
    ci = lax.broadcasted_iota(jnp.int32, (tg, tg), 1)
    low = ((ri >> log_c) == (ci >> log_c)) & (ci <= ri)
    causal = jnp.where(low, 1.0, 0.0)
    tri = causal.astype(BF16)
    lane_head = lax.broadcasted_iota(jnp.int32, (1, kw), 1) >> int(math.log2(GLA_KEY_DIM))
    head_mask = [lane_head == h for h in range(GLA_HEADS)]
    u_hi, u_lo = _split(up_ref[...])

    def stack_heads(x):
        return jnp.concatenate([jnp.where(head_mask[h], x, 0.0) for h in range(GLA_HEADS)], axis=0)

    def sequence(bb):
        m_hi, m_lo = _split(misc_ref[bb])
        g_pre = _dot(m_hi, u_hi) + _dot(m_lo, u_hi) + _dot(m_hi, u_lo) + bias_ref[...]
        yield
        g_log = (jnp.minimum(g_pre, 0.0) - jnp.log(1.0 + jnp.exp(-jnp.abs(g_pre)))) * (1.0 / GLA_GATE_NORM)
        g_hi, g_lo = _split(g_log)
        yield
        bcum = _dot(tri, g_hi) + _dot(tri, g_lo)
        yield
        b_last = jnp.concatenate([jnp.broadcast_to(bcum[(cc + 1) * c - 1:(cc + 1) * c, :], (c, kw))
                                  for cc in range(n_chunk)], axis=0)
        qf = q_ref[bb].astype(F32)
        kf = k_ref[bb].astype(F32)
        v = v_ref[bb]
        q_dec = qf * jnp.exp(bcum) * (GLA_KEY_DIM ** -0.5)
        k_dec = (kf * jnp.exp(-bcum)).astype(BF16)
        k_end = kf * jnp.exp(b_last - bcum)
        q_heads = [jnp.where(head_mask[h], q_dec, 0.0).astype(BF16) for h in range(GLA_HEADS)]
        yield
        scores = [_dot_nt(q_heads[h], k_dec) for h in range(GLA_HEADS)]
        yield
        a = [(scores[h] * causal).astype(BF16) for h in range(GLA_HEADS)]
        yield
        intra = [_dot(a[h], v[:, h * GLA_VAL_DIM:(h + 1) * GLA_VAL_DIM]) for h in range(GLA_HEADS)]
        st = st_ref[bb]
        inter = []
        for cc in range(n_chunk):
            r = slice(cc * c, (cc + 1) * c)
            inter.append(_dot_nt(stack_heads(q_dec[r]).astype(BF16), st.astype(BF16)))
            km = stack_heads(k_end[r]).astype(BF16)
            vst = jnp.concatenate([v[r, h * GLA_VAL_DIM:(h + 1) * GLA_VAL_DIM] for h in range(GLA_HEADS)],
                                  axis=0)
            st = st * jnp.exp(b_last[cc * c:cc * c + 1, :]) + _dot_tn(vst, km)
        st_ref[bb] = st
        yield
        for h in range(GLA_HEADS):
            vs = slice(h * GLA_VAL_DIM, (h + 1) * GLA_VAL_DIM)
            inter_h = jnp.concatenate([inter[cc][h * c:(h + 1) * c] for cc in range(n_chunk)], axis=0)
            gate = og_ref[bb, :, vs].astype(F32)
            o_ref[bb, :, vs] = (_rms(intra[h] + inter_h, g_ref[:, vs]) * (gate * _sigmoid(gate))).astype(BF16)
        yield

    _trace_skewed(sequence(bb) for bb in range(q_ref.shape[0]))


def _gla(gq, gk, gv, gog, misc, up_pad, bias, g):
    b, s, _ = gq.shape
    tg = min(GLA_TG, s)
    nb = GLA_NB if b % GLA_NB == 0 else 1

    def per_tile(a):
        return pl.BlockSpec((nb, tg, a.shape[2]), lambda i, j: (i, j, 0))

    def full(a):
        return pl.BlockSpec(a.shape, lambda i, j: (0, 0))

    return pl.pallas_call(
        _gla_kernel,
        grid=(b // nb, s // tg),
        in_specs=[per_tile(gq), per_tile(gk), per_tile(gv), per_tile(gog), per_tile(misc),
                  full(up_pad), full(bias), full(g)],
        out_specs=per_tile(gv),
        out_shape=jax.ShapeDtypeStruct(gv.shape, BF16),
        scratch_shapes=[pltpu.VMEM((nb, GLA_VAL_DIM, GLA_HEADS * GLA_KEY_DIM), F32)],
        compiler_params=_cparams(("parallel", "arbitrary")),
        name="gla",
    )(gq, gk, gv, gog, misc, up_pad, bias, g)


def _overlap_matrix(n_rows):
    n = np.arange(n_rows)[:, None] * CMP_STRIDE
    s = np.arange(LANES)[None, :] * SLC_BLOCK
    return ((n < s + SLC_BLOCK) & (n + CMP_BLOCK > s)).astype(np.float32)


def _layer(h, p, seq, batch, ffn1_pre_g, ffn1_post_g, ffn1_w_gate, ffn1_w_up, ffn1_w_down, mix_pre_g, mix_post_g,
           w_in, cmp_k_pe, cmp_k_w1, cmp_k_w2, cmp_v_pe, cmp_v_w1, cmp_v_w2, nsa_out_g, gla_gate_up, gla_gate_bias,
           gla_out_g, w_out, ffn2_pre_g, ffn2_post_g, ffn2_w_gate, ffn2_w_up, ffn2_w_down, ple_proj, ple_gate,
           ple_post_g):
    d = h.shape[1]
    row = lambda a: a.reshape(1, -1).astype(F32)
    bf = lambda a: a.astype(BF16)

    h1 = _ffn(h, row(ffn1_pre_g), row(ffn1_post_g), bf(ffn1_w_gate), bf(ffn1_w_up), bf(ffn1_w_down))

    o_gate = 512 + 6 * 128
    o_gq = o_gate + 3 * NSA_HEADS
    o_gk = o_gq + GLA_HEADS * GLA_KEY_DIM
    o_gv = o_gk + GLA_HEADS * GLA_KEY_DIM
    o_gd = o_gv + GLA_HEADS * GLA_VAL_DIM
    o_og = o_gd + GLA_GATE_RANK
    pad = jnp.zeros((d, LANES - 3 * NSA_HEADS - GLA_GATE_RANK), w_in.dtype)
    w_all = bf(jnp.concatenate([w_in[:, :o_gate], w_in[:, o_gate:o_gq], w_in[:, o_gd:o_og], pad,
                                w_in[:, o_gq:o_gd], w_in[:, o_og:]], axis=1))
    nq, kc_in, vc_in, ks, vs, kw, vw, misc, gq, gk, gv, gog = _proj(h1, row(mix_pre_g), w_all, seq)

    n_grp = seq // CMP_STRIDE
    grp_w = CMP_STRIDE * NSA_HEAD_DIM
    kc, vc = _compress(
        kc_in.reshape(batch, n_grp, grp_w), vc_in.reshape(batch, n_grp, grp_w),
        cmp_k_pe.reshape(2, grp_w).astype(F32), bf(cmp_k_w1.reshape(2 * grp_w, -1)), bf(cmp_k_w2),
        cmp_v_pe.reshape(2, grp_w).astype(F32), bf(cmp_v_w1.reshape(2 * grp_w, -1)), bf(cmp_v_w2))

    b3 = lambda a: a.reshape(batch, seq, a.shape[-1])
    ovt = jnp.asarray(_overlap_matrix(n_grp).T, BF16)
    o_nsa = _nsa(b3(nq), kc, vc, b3(ks), b3(vs), b3(kw), b3(vw), b3(misc), ovt, row(nsa_out_g))

    up_pad = jnp.zeros((LANES, GLA_HEADS * GLA_KEY_DIM), F32)
    up_pad = up_pad.at[_MISC_GDOWN:_MISC_GDOWN + GLA_GATE_RANK].set(gla_gate_up.astype(F32))
    o_gla = _gla(b3(gq), b3(gk), b3(gv), b3(gog), b3(misc), up_pad, row(gla_gate_bias), row(gla_out_g))

    w_o = bf(w_out)
    half = NSA_HEADS * NSA_HEAD_DIM
    return _ffn(h1, row(ffn2_pre_g), row(ffn2_post_g), bf(ffn2_w_gate), bf(ffn2_w_up), bf(ffn2_w_down),
                mix=(o_nsa.reshape(-1, half), o_gla.reshape(-1, half), w_o[:half], w_o[half:], row(mix_post_g)),
                ple=(p, bf(ple_proj), bf(ple_gate), row(ple_post_g)))


def kernel(x, p, ffn1_pre_g, ffn1_post_g, ffn1_w_gate, ffn1_w_up, ffn1_w_down, mix_pre_g, mix_post_g, w_in, cmp_k_pe, cmp_k_w1, cmp_k_w2, cmp_v_pe, cmp_v_w1, cmp_v_w2, nsa_out_g, gla_gate_up, gla_gate_bias, gla_out_g, w_out, ffn2_pre_g, ffn2_post_g, ffn2_w_gate, ffn2_w_up, ffn2_w_down, ple_proj, ple_gate, ple_post_g):
    batch, seq, d = x.shape
    depth = p.shape[0]
    assert seq % NSA_TK == 0 and seq >= WIN_SLAB and seq // SLC_BLOCK <= LANES
    h = x.reshape(batch * seq, d)
    params = (ffn1_pre_g, ffn1_post_g, ffn1_w_gate, ffn1_w_up, ffn1_w_down, mix_pre_g, mix_post_g, w_in, cmp_k_pe,
              cmp_k_w1, cmp_k_w2, cmp_v_pe, cmp_v_w1, cmp_v_w2, nsa_out_g, gla_gate_up, gla_gate_bias, gla_out_g,
              w_out, ffn2_pre_g, ffn2_post_g, ffn2_w_gate, ffn2_w_up, ffn2_w_down, ple_proj, ple_gate, ple_post_g)
    for i in range(depth):
        h = _layer(h, p[i].reshape(batch * seq, -1), seq, batch, *[a[i] for a in params])
    return h.reshape(batch, seq, d)
```
